```python
import jax, jax.numpy as jnp
from jax import lax
import numpy as np

D_MODEL = 4096
BATCH = 1
SEQ = 16384
DEPTH = 2

D_MIX = D_MODEL
D_MLSTM = D_MIX // 2
D_LRU = D_MIX - D_MLSTM
MLSTM_HEADS = 4
MLSTM_DV = D_MLSTM // MLSTM_HEADS
MLSTM_DK = MLSTM_DV // 2
MLSTM_QK = MLSTM_HEADS * MLSTM_DK
CHUNK = 64
GATE_SOFTCAP = 15.0
LRU_BLOCKS = 16
LRU_BLOCK = D_LRU // LRU_BLOCKS
LRU_C = 8.0
LRU_CONV = 4
D_FF = ((8 * D_MODEL // 3 + 255) // 256) * 256
FFN_CONV = 3
EPS = 1e-6

D_IN = 2 * MLSTM_QK + 2 * D_MLSTM + 2 * MLSTM_HEADS + 2 * D_LRU
SPLITS = (
    MLSTM_QK,
    2 * MLSTM_QK,
    2 * MLSTM_QK + D_MLSTM,
    2 * MLSTM_QK + 2 * D_MLSTM,
    2 * MLSTM_QK + 2 * D_MLSTM + MLSTM_HEADS,
    2 * MLSTM_QK + 2 * D_MLSTM + 2 * MLSTM_HEADS,
    2 * MLSTM_QK + 2 * D_MLSTM + 2 * MLSTM_HEADS + D_LRU,
)

kernel_name = "hymba_style_mlstm_rglru_convffn"


def rmsnorm(x, g):
    xf = x.astype(jnp.float32)
    y = xf * lax.rsqrt(jnp.mean(xf * xf, axis=-1, keepdims=True) + EPS)
    return (y * g.astype(jnp.float32)).astype(x.dtype)


def softcap(x, cap):
    return cap * jnp.tanh(x / cap)


def causal_dwconv(x, w, b):
    k = w.shape[0]
    out = lax.conv_general_dilated(
        x, w[:, None, :], window_strides=(1,), padding=[(k - 1, 0)],
        dimension_numbers=("NWC", "WIO", "NWC"), feature_group_count=x.shape[-1])
    return out + b


def mlstm_chunkwise(q, k, v, ig, fg):
    bsz, s = q.shape[0], q.shape[1]
    nc = s // CHUNK

    def to_chunks(t):
        t = t.reshape((bsz, nc, CHUNK) + t.shape[2:])
        return jnp.moveaxis(t, (1, 3), (0, 2))

    q = q * (MLSTM_DK ** -0.5)
    logf = jax.nn.log_sigmoid(fg)
    xs = (to_chunks(q), to_chunks(k), to_chunks(v), to_chunks(ig), to_chunks(logf))
    causal = jnp.tril(jnp.ones((CHUNK, CHUNK), dtype=bool))

    def step(carry, inp):
        c_prev, n_prev, m_prev = carry
        qc, kc, vc, ic, lfc = inp
        b = jnp.cumsum(lfc, axis=-1)
        log_d = jnp.where(causal, b[..., :, None] - b[..., None, :] + ic[..., None, :], -jnp.inf)
        m_inter = b + m_prev[..., None]
        m_t = jnp.maximum(m_inter, jnp.max(log_d, axis=-1))
        dmat = jnp.exp(log_d - m_t[..., None])
        scores = jnp.einsum("bhld,bhsd->bhls", qc, kc) * dmat
        inter = jnp.exp(m_inter - m_t)
        num = inter[..., None] * jnp.einsum("bhld,bhdv->bhlv", qc, c_prev) \
            + jnp.einsum("bhls,bhsv->bhlv", scores, vc)
        den = inter * jnp.einsum("bhld,bhd->bhl", qc, n_prev) + jnp.sum(scores, axis=-1)
        h = num / jnp.maximum(jnp.abs(den), jnp.exp(-m_t))[..., None]
        b_last = b[..., -1]
        w = b_last[..., None] - b + ic
        m_new = jnp.maximum(b_last + m_prev, jnp.max(w, axis=-1))
        decay = jnp.exp(b_last + m_prev - m_new)
        wk = jnp.exp(w - m_new[..., None])
        c_new = decay[..., None, None] * c_prev + jnp.einsum("bhl,bhld,bhlv->bhdv", wk, kc, vc)
        n_new = decay[..., None] * n_prev + jnp.einsum("bhl,bhld->bhd", wk, kc)
        return (c_new, n_new, m_new), h

    init = (jnp.zeros((bsz, MLSTM_HEADS, MLSTM_DK, MLSTM_DV), jnp.float32),
            jnp.zeros((bsz, MLSTM_HEADS, MLSTM_DK), jnp.float32),
            jnp.zeros((bsz, MLSTM_HEADS), jnp.float32))
    _, hs = lax.scan(step, init, xs)
    return jnp.moveaxis(hs, (0, 2), (1, 3)).reshape(bsz, s, MLSTM_HEADS, MLSTM_DV)


def rglru(x, w_a, b_a, w_x, b_x, lam):
    bsz, s = x.shape[0], x.shape[1]
    xf = x.astype(jnp.float32)
    xb = xf.reshape(bsz, s, LRU_BLOCKS, LRU_BLOCK)
    r = jax.nn.sigmoid(jnp.einsum("bsnd,nde->bsne", xb, w_a.astype(jnp.float32)).reshape(bsz, s, D_LRU) + b_a)
    i = jax.nn.sigmoid(jnp.einsum("bsnd,nde->bsne", xb, w_x.astype(jnp.float32)).reshape(bsz, s, D_LRU) + b_x)
    log_a = -LRU_C * r * jax.nn.softplus(-lam.astype(jnp.float32))
    a = jnp.exp(log_a)
    u = jnp.sqrt(-jnp.expm1(2.0 * log_a)) * (i * xf)

    def combine(left, right):
        a1, b1 = left
        a2, b2 = right
        return a1 * a2, a2 * b1 + b2

    _, h = lax.associative_scan(combine, (a, u), axis=1)
    return h


def setup_inputs(seed: int = 0) -> dict:
    key = jax.random.key(seed)
    ks = jax.random.split(key, 24)
    f32 = jnp.float32

    def nrm(k, shape, scale):
        return jax.random.normal(k, shape, f32) * scale

    def gain(k, n):
        return 1.0 + 0.02 * jax.random.normal(k, (DEPTH, n), f32)

    a_init = jax.random.uniform(ks[13], (DEPTH, D_LRU), f32, 0.9, 0.999)
    p = a_init ** (1.0 / LRU_C)
    lam = jnp.log(p) - jnp.log1p(-p)
    return {
        "x": jax.random.normal(ks[0], (BATCH, SEQ, D_MODEL), f32),
        "g_mix_pre": gain(ks[1], D_MODEL),
        "w_in": nrm(ks[2], (DEPTH, D_MODEL, D_IN), D_MODEL ** -0.5),
        "b_i": nrm(ks[3], (DEPTH, MLSTM_HEADS), 0.1),
        "b_f": jax.random.uniform(ks[4], (DEPTH, MLSTM_HEADS), f32, 3.0, 6.0),
        "g_mlstm_head": gain(ks[5], D_MLSTM),
        "lru_conv_w": nrm(ks[6], (DEPTH, LRU_CONV, D_LRU), LRU_CONV ** -0.5),
        "lru_conv_b": nrm(ks[7], (DEPTH, D_LRU), 0.01),
        "w_a": nrm(ks[8], (DEPTH, LRU_BLOCKS, LRU_BLOCK, LRU_BLOCK), LRU_BLOCK ** -0.5),
        "b_a": nrm(ks[9], (DEPTH, D_LRU), 0.01),
        "w_x": nrm(ks[10], (DEPTH, LRU_BLOCKS, LRU_BLOCK, LRU_BLOCK), LRU_BLOCK ** -0.5),
        "b_x": nrm(ks[11], (DEPTH, D_LRU), 0.01),
        "lam": lam,
        "w_out": nrm(ks[12], (DEPTH, D_MIX, D_MODEL), D_MIX ** -0.5),
        "g_mix_post": gain(ks[14], D_MODEL),
        "g_ffn_pre": gain(ks[15], D_MODEL),
        "w_gate": nrm(ks[16], (DEPTH, D_MODEL, D_FF), D_MODEL ** -0.5),
        "w_up": nrm(ks[17], (DEPTH, D_MODEL, D_FF), D_MODEL ** -0.5),
        "ffn_conv_w": nrm(ks[18], (DEPTH, FFN_CONV, D_FF), FFN_CONV ** -0.5),
        "ffn_conv_b": nrm(ks[19], (DEPTH, D_FF), 0.01),
        "w_down": nrm(ks[20], (DEPTH, D_FF, D_MODEL), D_FF ** -0.5),
        "g_ffn_post": gain(ks[21], D_MODEL),
    }


def reference(x, g_mix_pre, w_in, b_i, b_f, g_mlstm_head, lru_conv_w, lru_conv_b, w_a, b_a,
              w_x, b_x, lam, w_out, g_mix_post, g_ffn_pre, w_gate, w_up, ffn_conv_w, ffn_conv_b,
              w_down, g_ffn_post):
    bsz, s, _ = x.shape
    for l in range(DEPTH):
        h = rmsnorm(x, g_mix_pre[l])
        z = h @ w_in[l]
        zq, zk, zv, zo, zi, zf, zx, zy = jnp.split(z, SPLITS, axis=-1)
        q = zq.reshape(bsz, s, MLSTM_HEADS, MLSTM_DK).astype(jnp.float32)
        k = zk.reshape(bsz, s, MLSTM_HEADS, MLSTM_DK).astype(jnp.float32)
        v = zv.reshape(bsz, s, MLSTM_HEADS, MLSTM_DV).astype(jnp.float32)
        ig = softcap(zi.astype(jnp.float32) + b_i[l].astype(jnp.float32), GATE_SOFTCAP)
        fg = softcap(zf.astype(jnp.float32) + b_f[l].astype(jnp.float32), GATE_SOFTCAP)
        hm = mlstm_chunkwise(q, k, v, ig, fg)
        hm = rmsnorm(hm, g_mlstm_head[l].reshape(MLSTM_HEADS, MLSTM_DV))
        hm = hm.reshape(bsz, s, D_MLSTM) * jax.nn.sigmoid(zo.astype(jnp.float32))
        xc = causal_dwconv(zx, lru_conv_w[l], lru_conv_b[l])
        hl = rglru(xc, w_a[l], b_a[l], w_x[l], b_x[l], lam[l]) \
            * jax.nn.gelu(zy.astype(jnp.float32), approximate=True)
        mix = jnp.concatenate([hm, hl], axis=-1).astype(x.dtype) @ w_out[l]
        x = x + rmsnorm(mix, g_mix_post[l])
        h = rmsnorm(x, g_ffn_pre[l])
        gate = causal_dwconv(h @ w_gate[l], ffn_conv_w[l], ffn_conv_b[l])
        ff = (jax.nn.gelu(gate, approximate=True) * (h @ w_up[l])) @ w_down[l]
        x = x + rmsnorm(ff, g_ffn_post[l])
    return x
```

```python
import functools
import math

import jax
import jax.numpy as jnp
from jax import lax
from jax.experimental import pallas as pl
from jax.experimental.pallas import tpu as pltpu

EPS = 1e-6
GATE_SOFTCAP = 15.0
LRU_C = 8.0
LANES = 128
SUBLANES = 8
VMEM_LIMIT_BYTES = 56 * 1024 * 1024
MLSTM_CHUNK = 256
LRU_TILE = 256
FF_TILE = 512
ROW_TILE = 1024

BF16 = jnp.bfloat16
F32 = jnp.float32


def _params(n_axes):
    return pltpu.CompilerParams(
        dimension_semantics=("arbitrary",) * n_axes,
        vmem_limit_bytes=VMEM_LIMIT_BYTES)


def _gelu_tanh(y):
    c = math.sqrt(2.0 / math.pi)
    return 0.5 * y * (1.0 + jnp.tanh(c * (y + 0.044715 * (y * y * y))))


def _sigmoid(y):
    return 0.5 * jnp.tanh(0.5 * y) + 0.5


def _softplus(y):
    return jnp.maximum(y, 0.0) + jnp.log1p(jnp.exp(-jnp.abs(y)))


def _rms_scale(v):
    return lax.rsqrt(jnp.mean(v * v, axis=-1, keepdims=True) + EPS)


def _norm_kernel(x_ref, g_ref, h_ref):
    x = x_ref[...]
    h_ref[...] = (x * _rms_scale(x) * g_ref[...]).astype(h_ref.dtype)


def _norm(x, g, tm):
    s, d = x.shape
    return pl.pallas_call(
        _norm_kernel,
        grid=(s // tm,),
        in_specs=[pl.BlockSpec((tm, d), lambda i: (i, 0)),
                  pl.BlockSpec((1, d), lambda i: (0, 0))],
        out_specs=pl.BlockSpec((tm, d), lambda i: (i, 0)),
        out_shape=jax.ShapeDtypeStruct((s, d), BF16),
        compiler_params=_params(1),
        name="rmsnorm",
    )(x, g)


def _resnorm_kernel(x_ref, y_ref, gp_ref, gn_ref, xo_ref, h_ref):
    y = y_ref[...]
    xn = x_ref[...] + y * _rms_scale(y) * gp_ref[...]
    xo_ref[...] = xn
    h_ref[...] = (xn * _rms_scale(xn) * gn_ref[...]).astype(h_ref.dtype)


def _resnorm_last_kernel(x_ref, y_ref, gp_ref, xo_ref):
    y = y_ref[...]
    xo_ref[...] = x_ref[...] + y * _rms_scale(y) * gp_ref[...]


def _resnorm(x, y, g_post, g_next, tm):
    s, d = x.shape
    row = pl.BlockSpec((tm, d), lambda i: (i, 0))
    vec = pl.BlockSpec((1, d), lambda i: (0, 0))
    if g_next is None:
        return pl.pallas_call(
            _resnorm_last_kernel, grid=(s // tm,),
            in_specs=[row, row, vec], out_specs=row,
            out_shape=jax.ShapeDtypeStruct((s, d), F32),
            compiler_params=_params(1), name="resnorm_last",
        )(x, y, g_post), None
    return pl.pallas_call(
        _resnorm_kernel, grid=(s // tm,),
        in_specs=[row, row, vec, vec], out_specs=[row, row],
        out_shape=[jax.ShapeDtypeStruct((s, d), F32), jax.ShapeDtypeStruct((s, d), BF16)],
        compiler_params=_params(1), name="resnorm",
    )(x, y, g_post, g_next)


def _mm_kernel(*refs, n_pairs):
    o_ref = refs[-1]
    acc = None
    for p in range(n_pairs):
        d = jnp.dot(refs[2 * p][...], refs[2 * p + 1][...], preferred_element_type=F32)
        acc = d if acc is None else acc + d
    o_ref[...] = acc.astype(o_ref.dtype)


def _matmul(pairs, out_dtype, tm, tn, name):
    s = pairs[0][0].shape[0]
    n = pairs[0][1].shape[1]
    in_specs, args = [], []
    for a, w in pairs:
        k = a.shape[1]
        in_specs += [pl.BlockSpec((tm, k), lambda i, j: (i, 0)),
                     pl.BlockSpec((k, tn), lambda i, j: (0, j))]
        args += [a, w]
    return pl.pallas_call(
        functools.partial(_mm_kernel, n_pairs=len(pairs)),
        grid=(s // tm, n // tn),
        in_specs=in_specs,
        out_specs=pl.BlockSpec((tm, tn), lambda i, j: (i, j)),
        out_shape=jax.ShapeDtypeStruct((s, n), out_dtype),
        compiler_params=_params(2),
        name=name,
    )(*args)


def _mlstm_kernel(q_ref, k_ref, v_ref, o_ref, gt_ref, gb_ref, gh_ref, out_ref,
                  c_ref, n_ref, m_ref, *, heads, dk, dv):
    chunk = q_ref.shape[0]
    scale = dk ** -0.5

    @pl.when(pl.program_id(0) == 0)
    def _():
        c_ref[...] = jnp.zeros_like(c_ref)
        n_ref[...] = jnp.zeros_like(n_ref)
        m_ref[...] = jnp.zeros_like(m_ref)

    pre = gt_ref[...] + gb_ref[...]
    capped = GATE_SOFTCAP * jnp.tanh(pre * (1.0 / GATE_SOFTCAP))
    logf = -_softplus(-capped)
    rows = lax.broadcasted_iota(jnp.int32, (chunk, chunk), 0)
    cols = lax.broadcasted_iota(jnp.int32, (chunk, chunk), 1)
    causal = rows >= cols
    tri = jnp.where(causal, 1.0, 0.0).astype(BF16)
    p0 = logf.astype(BF16)
    r1 = logf - p0.astype(F32)
    p1 = r1.astype(BF16)
    p2 = (r1 - p1.astype(F32)).astype(BF16)
    bcum = (jnp.dot(tri, p0, preferred_element_type=F32)
            + jnp.dot(tri, p1, preferred_element_type=F32)
            + jnp.dot(tri, p2, preferred_element_type=F32))
    wmat = capped - pltpu.roll(bcum, LANES - heads, axis=1)
    wmat_t = wmat.T

    for h in range(heads):
        q = q_ref[:, h * dk:(h + 1) * dk]
        k = k_ref[:, h * dk:(h + 1) * dk]
        v = v_ref[:, h * dv:(h + 1) * dv]
        b_col = bcum[:, heads + h:heads + h + 1]
        w_col = wmat[:, h:h + 1]
        w_row = wmat_t[h:h + 1, :]
        m_prev = m_ref[h, :, 0:1]
        c_prev = c_ref[h]
        n_prev = n_ref[h]

        log_d = jnp.where(causal, b_col + w_row, -jnp.inf)
        m_inter = b_col + m_prev
        m_t = jnp.maximum(m_inter, jnp.max(log_d, axis=1, keepdims=True))
        dmat = jnp.exp(log_d - m_t)
        s_qk = lax.dot_general(q, k, (((1,), (1,)), ((), ())), preferred_element_type=F32)
        scores = s_qk * dmat * scale
        inter = jnp.exp(m_inter - m_t) * scale
        num = inter * jnp.dot(q, c_prev.astype(BF16), preferred_element_type=F32) \
            + jnp.dot(scores.astype(BF16), v, preferred_element_type=F32)
        den = inter * jnp.sum(q.astype(F32) * n_prev, axis=1, keepdims=True) \
            + jnp.sum(scores, axis=1, keepdims=True)
        hval = num * (1.0 / jnp.maximum(jnp.abs(den), jnp.exp(-m_t)))
        hval = hval * _rms_scale(hval) * gh_ref[:, h * dv:(h + 1) * dv]
        gate = _sigmoid(o_ref[:, h * dv:(h + 1) * dv].astype(F32))
        out_ref[:, h * dv:(h + 1) * dv] = (hval * gate).astype(out_ref.dtype)

        b_last = b_col[chunk - 1:chunk, :]
        m_new = jnp.maximum(b_last + m_prev,
                            b_last + jnp.max(w_col, axis=0, keepdims=True))
        decay = jnp.exp(b_last + m_prev - m_new)
        wk = jnp.exp(b_last + w_col - m_new)
        kw = k.astype(F32) * wk
        c_ref[h] = decay * c_prev + lax.dot_general(
            kw.astype(BF16), v, (((0,), (0,)), ((), ())), preferred_element_type=F32)
        n_ref[h] = decay * n_prev + jnp.sum(kw, axis=0, keepdims=True)
        m_ref[h] = jnp.broadcast_to(m_new, (1, LANES))


def _mlstm(z, gates, gate_bias, g_head, heads, dk, dv, chunk):
    s = z.shape[0]
    qk = heads * dk
    dm = heads * dv
    return pl.pallas_call(
        functools.partial(_mlstm_kernel, heads=heads, dk=dk, dv=dv),
        grid=(s // chunk,),
        in_specs=[pl.BlockSpec((chunk, qk), lambda c: (c, 0)),
                  pl.BlockSpec((chunk, qk), lambda c: (c, 1)),
                  pl.BlockSpec((chunk, dm), lambda c: (c, 1)),
                  pl.BlockSpec((chunk, dm), lambda c: (c, 2)),
                  pl.BlockSpec((chunk, LANES), lambda c: (c, 0)),
                  pl.BlockSpec((1, LANES), lambda c: (0, 0)),
                  pl.BlockSpec((1, dm), lambda c: (0, 0))],
        out_specs=pl.BlockSpec((chunk, dm), lambda c: (c, 0)),
        out_shape=jax.ShapeDtypeStruct((s, dm), BF16),
        scratch_shapes=[pltpu.VMEM((heads, dk, dv), F32),
                        pltpu.VMEM((heads, 1, dk), F32),
                        pltpu.VMEM((heads, 1, LANES), F32)],
        compiler_params=_params(1),
        name="mlstm",
    )(z, z, z, z, gates, gate_bias, g_head)


def _rglru_kernel(zx_ref, zy_ref, cw_ref, cb_ref, w_ref, ba_ref, bx_ref, lam_ref, out_ref,
                  xs_ref, a_ref, u_ref, h_ref, hc_ref, *, blocks, conv_k):
    tile, width = zx_ref.shape
    blk = width // blocks

    @pl.when(pl.program_id(0) == 0)
    def _():
        xs_ref[0:SUBLANES, :] = jnp.zeros((SUBLANES, width), F32)
        hc_ref[...] = jnp.zeros_like(hc_ref)

    x = zx_ref[...].astype(F32)
    xs_ref[SUBLANES:SUBLANES + tile, :] = x
    xc = cb_ref[...] + cw_ref[conv_k - 1:conv_k, :] * x
    for j in range(conv_k - 1):
        off = SUBLANES - (conv_k - 1) + j
        xc = xc + cw_ref[j:j + 1, :] * xs_ref[off:off + tile, :]
    xs_ref[0:SUBLANES, :] = x[tile - SUBLANES:, :]

    sp = _softplus(-lam_ref[...])
    for n in range(blocks):
        sl = slice(n * blk, (n + 1) * blk)
        xb = xc[:, sl]
        ri = jnp.dot(xb.astype(BF16), w_ref[n], preferred_element_type=F32)
        r = _sigmoid(ri[:, :blk] + ba_ref[:, sl])
        ig = _sigmoid(ri[:, blk:] + bx_ref[:, sl])
        log_a = (-LRU_C) * r * sp[:, sl]
        a_ref[:, sl] = jnp.exp(log_a)
        th = jnp.tanh(log_a)
        u_ref[:, sl] = jnp.sqrt(-2.0 * th / (1.0 - th)) * (ig * xb)

    rowid = lax.broadcasted_iota(jnp.int32, (SUBLANES, width), 0)

    def group(g, hc):
        r0 = pl.multiple_of(g * SUBLANES, SUBLANES)
        a = a_ref[pl.ds(r0, SUBLANES), :]
        u = u_ref[pl.ds(r0, SUBLANES), :]
        for d in (1, 2, 4):
            keep = rowid >= d
            a_sh = jnp.where(keep, pltpu.roll(a, d, axis=0), 1.0)
            u_sh = jnp.where(keep, pltpu.roll(u, d, axis=0), 0.0)
            u = a * u_sh + u
            a = a * a_sh
        hv = u + a * hc
        h_ref[pl.ds(r0, SUBLANES), :] = hv
        return jnp.broadcast_to(hv[SUBLANES - 1:SUBLANES, :], (SUBLANES, width))

    hc = lax.fori_loop(0, tile // SUBLANES, group, hc_ref[...])
    hc_ref[...] = hc
    out_ref[...] = (h_ref[...] * _gelu_tanh(zy_ref[...].astype(F32))).astype(out_ref.dtype)


def _rglru(z, zx_block, conv_w, conv_b, w_cat, b_a, b_x, lam, tile):
    s = z.shape[0]
    blocks, blk, _ = w_cat.shape
    width = blocks * blk
    conv_k = conv_w.shape[0]
    vec = pl.BlockSpec((1, width), lambda i: (0, 0))
    return pl.pallas_call(
        functools.partial(_rglru_kernel, blocks=blocks, conv_k=conv_k),
        grid=(s // tile,),
        in_specs=[pl.BlockSpec((tile, width), lambda i: (i, zx_block)),
                  pl.BlockSpec((tile, width), lambda i: (i, zx_block + 1)),
                  pl.BlockSpec((conv_k, width), lambda i: (0, 0)),
                  vec,
                  pl.BlockSpec((blocks, blk, 2 * blk), lambda i: (0, 0, 0)),
                  vec, vec, vec],
        out_specs=pl.BlockSpec((tile, width), lambda i: (i, 0)),
        out_shape=jax.ShapeDtypeStruct((s, width), BF16),
        scratch_shapes=[pltpu.VMEM((tile + SUBLANES, width), F32),
                        pltpu.VMEM((tile, width), F32),
                        pltpu.VMEM((tile, width), F32),
                        pltpu.VMEM((tile, width), F32),
                        pltpu.VMEM((SUBLANES, width), F32)],
        compiler_params=_params(1),
        name="rglru",
    )(z, z, conv_w, conv_b, w_cat, b_a, b_x, lam)


def _ffn_up_kernel(h_ref, wg_ref, wu_ref, cw_ref, cb_ref, out_ref, gs_ref, carry_ref, *, conv_k):
    i = pl.program_id(0)
    j = pl.program_id(1)
    tm = h_ref.shape[0]
    h = h_ref[...]
    gate = jnp.dot(h, wg_ref[...], preferred_element_type=F32)
    up = jnp.dot(h, wu_ref[...], preferred_element_type=F32)

    @pl.when(i == 0)
    def _():
        gs_ref[0:SUBLANES, :] = jnp.zeros((SUBLANES, gate.shape[1]), F32)

    @pl.when(i > 0)
    def _():
        gs_ref[0:SUBLANES, :] = carry_ref[j]

    gs_ref[SUBLANES:SUBLANES + tm, :] = gate
    carry_ref[j] = gate[tm - SUBLANES:, :]
    conv = cb_ref[...] + cw_ref[conv_k - 1:conv_k, :] * gate
    for t in range(conv_k - 1):
        off = SUBLANES - (conv_k - 1) + t
        conv = conv + cw_ref[t:t + 1, :] * gs_ref[off:off + tm, :]
    out_ref[...] = (_gelu_tanh(conv) * up).astype(out_ref.dtype)


def _ffn_up(h, w_gate, w_up, conv_w, conv_b, tm, tn):
    s, d = h.shape
    ff = w_gate.shape[1]
    conv_k = conv_w.shape[0]
    return pl.pallas_call(
        functools.partial(_ffn_up_kernel, conv_k=conv_k),
        grid=(s // tm, ff // tn),
        in_specs=[pl.BlockSpec((tm, d), lambda i, j: (i, 0)),
                  pl.BlockSpec((d, tn), lambda i, j: (0, j)),
                  pl.BlockSpec((d, tn), lambda i, j: (0, j)),
                  pl.BlockSpec((conv_k, tn), lambda i, j: (0, j)),
                  pl.BlockSpec((1, tn), lambda i, j: (0, j))],
        out_specs=pl.BlockSpec((tm, tn), lambda i, j: (i, j)),
        out_shape=jax.ShapeDtypeStruct((s, ff), BF16),
        scratch_shapes=[pltpu.VMEM((tm + SUBLANES, tn), F32),
                        pltpu.VMEM((ff // tn, SUBLANES, tn), F32)],
        compiler_params=_params(2),
        name="ffn_up",
    )(h, w_gate, w_up, conv_w, conv_b)


def _ffn_down_kernel(a_ref, w_ref, o_ref):
    @pl.when(pl.program_id(1) == 0)
    def _():
        o_ref[...] = jnp.zeros_like(o_ref)

    o_ref[...] += jnp.dot(a_ref[...], w_ref[...], preferred_element_type=F32)


def _ffn_down(hid, w_down, tm, tk):
    s, ff = hid.shape
    d = w_down.shape[1]
    return pl.pallas_call(
        _ffn_down_kernel,
        grid=(s // tm, ff // tk),
        in_specs=[pl.BlockSpec((tm, tk), lambda i, k: (i, k)),
                  pl.BlockSpec((tk, d), lambda i, k: (k, 0))],
        out_specs=pl.BlockSpec((tm, d), lambda i, k: (i, 0)),
        out_shape=jax.ShapeDtypeStruct((s, d), F32),
        compiler_params=_params(2),
        name="ffn_down",
    )(hid, w_down)


def _pad_to(a, axis, size):
    pad = [(0, 0)] * a.ndim
    pad[axis] = (0, size - a.shape[axis])
    return jnp.pad(a, pad)


def kernel(x, g_mix_pre, w_in, b_i, b_f, g_mlstm_head, lru_conv_w, lru_conv_b, w_a, b_a, w_x, b_x,
           lam, w_out, g_mix_post, g_ffn_pre, w_gate, w_up, ffn_conv_w, ffn_conv_b, w_down, g_ffn_post):
    bsz, seq, d_model = x.shape
    depth = w_in.shape[0]
    heads = b_i.shape[1]
    lru_blocks, lru_blk = w_a.shape[1], w_a.shape[2]
    d_lru = lru_blocks * lru_blk
    d_mlstm = w_out.shape[1] - d_lru
    dv = d_mlstm // heads
    dk = dv // 2
    qk = heads * dk
    d_ff = w_gate.shape[2]
    gate_off = 2 * qk + 2 * d_mlstm
    assert w_in.shape[2] == gate_off + 2 * heads + 2 * d_lru
    assert d_mlstm == 2 * qk and d_lru == d_mlstm and 2 * heads <= LANES
    s = bsz * seq
    assert bsz == 1, "the recurrences carry state across row tiles of a single sequence"

    tm = min(ROW_TILE, s)
    tn_in = min(ROW_TILE, qk)
    chunk = min(MLSTM_CHUNK, s)
    lru_tile = min(LRU_TILE, s)
    ff_pad = -(-d_ff // FF_TILE) * FF_TILE
    norm_tm = min(256, s)

    xf = x.reshape(s, d_model)
    row = lambda v: v.reshape(1, -1)
    h = _norm(xf, row(g_mix_pre[0]), norm_tm)
    for l in range(depth):
        w_main = jnp.concatenate([w_in[l][:, :gate_off], w_in[l][:, gate_off + 2 * heads:]],
                                 axis=1).astype(BF16)
        w_g = _pad_to(w_in[l][:, gate_off:gate_off + 2 * heads], 1, LANES).astype(BF16)
        gate_bias = _pad_to(jnp.concatenate([b_i[l], b_f[l]]), 0, LANES).reshape(1, LANES)
        w_cat = jnp.concatenate([w_a[l], w_x[l]], axis=-1).astype(BF16)
        w_o = w_out[l].astype(BF16)
        w_gt = _pad_to(w_gate[l], 1, ff_pad).astype(BF16)
        w_u = _pad_to(w_up[l], 1, ff_pad).astype(BF16)
        w_d = _pad_to(w_down[l], 0, ff_pad).astype(BF16)
        f_cw = _pad_to(ffn_conv_w[l], 1, ff_pad)
        f_cb = _pad_to(ffn_conv_b[l], 0, ff_pad).reshape(1, ff_pad)

        z = _matmul([(h, w_main)], BF16, tm, tn_in, "in_proj")
        gates = _matmul([(h, w_g)], F32, tm, LANES, "gate_proj")
        hm = _mlstm(z, gates, gate_bias, row(g_mlstm_head[l]), heads, dk, dv, chunk)
        hl = _rglru(z, (2 * qk + 2 * d_mlstm) // d_lru, lru_conv_w[l], row(lru_conv_b[l]), w_cat,
                    row(b_a[l]), row(b_x[l]), row(lam[l]), lru_tile)
        mix = _matmul([(hm, w_o[:d_mlstm]), (hl, w_o[d_mlstm:])], F32, tm, tn_in, "out_proj")
        xf, h = _resnorm(xf, mix, row(g_mix_post[l]), row(g_ffn_pre[l]), norm_tm)

        hid = _ffn_up(h, w_gt, w_u, f_cw, f_cb, tm, FF_TILE)
        ff = _ffn_down(hid, w_d, tm, FF_TILE)
        g_next = row(g_mix_pre[l + 1]) if l + 1 < depth else None
        xf, h = _resnorm(xf, ff, row(g_ffn_post[l]), g_next, norm_tm)
    return xf.reshape(bsz, seq, d_model)
```

```python
import functools
import math

import jax
import jax.numpy as jnp
from jax import lax
from jax.experimental import pallas as pl
from jax.experimental.pallas import tpu as pltpu

EPS = 1e-6
GATE_SOFTCAP = 15.0
LRU_C = 8.0
LANES = 128
SUBLANES = 8
VMEM_LIMIT_BYTES = 56 * 1024 * 1024
MLSTM_CHUNK = 256
LRU_TILE = 256
FF_TILE = 256
ROW_TILE = 1024
MM_TILE_N = 512

BF16 = jnp.bfloat16
F32 = jnp.float32


def _params(n_axes):
    return pltpu.CompilerParams(
        dimension_semantics=("arbitrary",) * n_axes,
        vmem_limit_bytes=VMEM_LIMIT_BYTES)


def _gelu_tanh(y):
    c = math.sqrt(2.0 / math.pi)
    return 0.5 * y * (1.0 + jnp.tanh(c * (y + 0.044715 * (y * y * y))))


def _sigmoid(y):
    return 0.5 * jnp.tanh(0.5 * y) + 0.5


def _softplus(y):
    return jnp.maximum(y, 0.0) + jnp.log1p(jnp.exp(-jnp.abs(y)))


def _rms_scale(v):
    return lax.rsqrt(jnp.mean(v * v, axis=-1, keepdims=True) + EPS)


def _norm_kernel(x_ref, g_ref, h_ref):
    x = x_ref[...]
    h_ref[...] = (x * _rms_scale(x) * g_ref[...]).astype(h_ref.dtype)


def _norm(x, g, tm):
    s, d = x.shape
    return pl.pallas_call(
        _norm_kernel,
        grid=(s // tm,),
        in_specs=[pl.BlockSpec((tm, d), lambda i: (i, 0)),
                  pl.BlockSpec((1, d), lambda i: (0, 0))],
        out_specs=pl.BlockSpec((tm, d), lambda i: (i, 0)),
        out_shape=jax.ShapeDtypeStruct((s, d), BF16),
        compiler_params=_params(1),
        name="rmsnorm",
    )(x, g)


def _resnorm_kernel(x_ref, y_ref, gp_ref, gn_ref, xo_ref, h_ref):
    y = y_ref[...].astype(F32)
    xn = x_ref[...] + y * _rms_scale(y) * gp_ref[...]
    xo_ref[...] = xn
    h_ref[...] = (xn * _rms_scale(xn) * gn_ref[...]).astype(h_ref.dtype)


def _resnorm_last_kernel(x_ref, y_ref, gp_ref, xo_ref):
    y = y_ref[...].astype(F32)
    xo_ref[...] = x_ref[...] + y * _rms_scale(y) * gp_ref[...]


def _resnorm(x, y, g_post, g_next, tm):
    s, d = x.shape
    row = pl.BlockSpec((tm, d), lambda i: (i, 0))
    vec = pl.BlockSpec((1, d), lambda i: (0, 0))
    if g_next is None:
        return pl.pallas_call(
            _resnorm_last_kernel, grid=(s // tm,),
            in_specs=[row, row, vec], out_specs=row,
            out_shape=jax.ShapeDtypeStruct((s, d), F32),
            compiler_params=_params(1), name="resnorm_last",
        )(x, y, g_post), None
    return pl.pallas_call(
        _resnorm_kernel, grid=(s // tm,),
        in_specs=[row, row, vec, vec], out_specs=[row, row],
        out_shape=[jax.ShapeDtypeStruct((s, d), F32), jax.ShapeDtypeStruct((s, d), BF16)],
        compiler_params=_params(1), name="resnorm",
    )(x, y, g_post, g_next)


def _mm_kernel(*refs, n_pairs):
    o_ref = refs[-1]
    acc = None
    for p in range(n_pairs):
        d = jnp.dot(refs[2 * p][...], refs[2 * p + 1][...].astype(BF16), preferred_element_type=F32)
        acc = d if acc is None else acc + d
    o_ref[...] = acc.astype(o_ref.dtype)


def _matmul(pairs, n_out, out_dtype, tm, tn, name):
    s = pairs[0][0].shape[0]
    in_specs, args = [], []
    for a, w, layer, kblk in pairs:
        k = a.shape[1]
        in_specs += [pl.BlockSpec((tm, k), lambda i, j: (i, 0)),
                     pl.BlockSpec((None, k, tn), lambda i, j, layer=layer, kblk=kblk: (layer, kblk, j))]
        args += [a, w]
    return pl.pallas_call(
        functools.partial(_mm_kernel, n_pairs=len(pairs)),
        grid=(s // tm, n_out // tn),
        in_specs=in_specs,
        out_specs=pl.BlockSpec((tm, tn), lambda i, j: (i, j)),
        out_shape=jax.ShapeDtypeStruct((s, n_out), out_dtype),
        compiler_params=_params(2),
        name=name,
    )(*args)


def _mlstm_kernel(q_ref, k_ref, v_ref, o_ref, gt_ref, gb_ref, gh_ref, out_ref,
                  c_ref, n_ref, m_ref, *, heads, dk, dv):
    chunk = q_ref.shape[0]
    scale = dk ** -0.5

    @pl.when(pl.program_id(0) == 0)
    def _():
        c_ref[...] = jnp.zeros_like(c_ref)
        n_ref[...] = jnp.zeros_like(n_ref)
        m_ref[...] = jnp.zeros_like(m_ref)

    pre = gt_ref[...] + gb_ref[...]
    capped = GATE_SOFTCAP * jnp.tanh(pre * (1.0 / GATE_SOFTCAP))
    logf = -_softplus(-capped)
    rows = lax.broadcasted_iota(jnp.int32, (chunk, chunk), 0)
    cols = lax.broadcasted_iota(jnp.int32, (chunk, chunk), 1)
    causal = rows >= cols
    tri = jnp.where(causal, 1.0, 0.0).astype(BF16)
    p0 = logf.astype(BF16)
    r1 = logf - p0.astype(F32)
    p1 = r1.astype(BF16)
    p2 = (r1 - p1.astype(F32)).astype(BF16)
    bcum = (jnp.dot(tri, p0, preferred_element_type=F32)
            + jnp.dot(tri, p1, preferred_element_type=F32)
            + jnp.dot(tri, p2, preferred_element_type=F32))
    wmat = capped - pltpu.roll(bcum, LANES - heads, axis=1)
    wmat_t = wmat.T

    for h in range(heads):
        q = q_ref[:, h * dk:(h + 1) * dk]
        k = k_ref[:, h * dk:(h + 1) * dk]
        v = v_ref[:, h * dv:(h + 1) * dv]
        b_col = bcum[:, heads + h:heads + h + 1]
        w_col = wmat[:, h:h + 1]
        w_row = wmat_t[h:h + 1, :]
        m_prev = m_ref[h, :, 0:1]
        c_prev = c_ref[h]
        n_prev = n_ref[h]

        log_d = jnp.where(causal, b_col + w_row, -jnp.inf)
        m_inter = b_col + m_prev
        m_t = jnp.maximum(m_inter, jnp.max(log_d, axis=1, keepdims=True))
        dmat = jnp.exp(log_d - m_t)
        s_qk = lax.dot_general(q, k, (((1,), (1,)), ((), ())), preferred_element_type=F32)
        scores = s_qk * dmat * scale
        inter = jnp.exp(m_inter - m_t) * scale
        num = inter * jnp.dot(q, c_prev.astype(BF16), preferred_element_type=F32) \
            + jnp.dot(scores.astype(BF16), v, preferred_element_type=F32)
        den = inter * jnp.sum(q.astype(F32) * n_prev, axis=1, keepdims=True) \
            + jnp.sum(scores, axis=1, keepdims=True)
        hval = num * (1.0 / jnp.maximum(jnp.abs(den), jnp.exp(-m_t)))
        hval = hval * _rms_scale(hval) * gh_ref[:, h * dv:(h + 1) * dv]
        gate = _sigmoid(o_ref[:, h * dv:(h + 1) * dv].astype(F32))
        out_ref[:, h * dv:(h + 1) * dv] = (hval * gate).astype(out_ref.dtype)

        b_last = b_col[chunk - 1:chunk, :]
        m_new = jnp.maximum(b_last + m_prev,
                            b_last + jnp.max(w_col, axis=0, keepdims=True))
        decay = jnp.exp(b_last + m_prev - m_new)
        wk = jnp.exp(b_last + w_col - m_new)
        kw = k.astype(F32) * wk
        c_ref[h] = decay * c_prev + lax.dot_general(
            kw.astype(BF16), v, (((0,), (0,)), ((), ())), preferred_element_type=F32)
        n_ref[h] = decay * n_prev + jnp.sum(kw, axis=0, keepdims=True)
        m_ref[h] = jnp.broadcast_to(m_new, (1, LANES))


def _mlstm(z, gates, gate_bias, g_head, heads, dk, dv, chunk):
    s = z.shape[0]
    qk = heads * dk
    dm = heads * dv
    return pl.pallas_call(
        functools.partial(_mlstm_kernel, heads=heads, dk=dk, dv=dv),
        grid=(s // chunk,),
        in_specs=[pl.BlockSpec((chunk, qk), lambda c: (c, 0)),
                  pl.BlockSpec((chunk, qk), lambda c: (c, 1)),
                  pl.BlockSpec((chunk, dm), lambda c: (c, 1)),
                  pl.BlockSpec((chunk, dm), lambda c: (c, 2)),
                  pl.BlockSpec((chunk, LANES), lambda c: (c, 0)),
                  pl.BlockSpec((1, LANES), lambda c: (0, 0)),
                  pl.BlockSpec((1, dm), lambda c: (0, 0))],
        out_specs=pl.BlockSpec((chunk, dm), lambda c: (c, 0)),
        out_shape=jax.ShapeDtypeStruct((s, dm), BF16),
        scratch_shapes=[pltpu.VMEM((heads, dk, dv), F32),
                        pltpu.VMEM((heads, 1, dk), F32),
                        pltpu.VMEM((heads, 1, LANES), F32)],
        compiler_params=_params(1),
        name="mlstm",
    )(z, z, z, z, gates, gate_bias, g_head)


def _rglru_kernel(zx_ref, zy_ref, cw_ref, cb_ref, w_ref, ba_ref, bx_ref, lam_ref, out_ref,
                  xs_ref, a_ref, u_ref, h_ref, hc_ref, *, blocks, conv_k):
    tile, width = zx_ref.shape
    blk = width // blocks

    @pl.when(pl.program_id(0) == 0)
    def _():
        xs_ref[0:SUBLANES, :] = jnp.zeros((SUBLANES, width), F32)
        hc_ref[...] = jnp.zeros_like(hc_ref)

    x = zx_ref[...].astype(F32)
    xs_ref[SUBLANES:SUBLANES + tile, :] = x
    xc = cb_ref[...] + cw_ref[conv_k - 1:conv_k, :] * x
    for j in range(conv_k - 1):
        off = SUBLANES - (conv_k - 1) + j
        xc = xc + cw_ref[j:j + 1, :] * xs_ref[off:off + tile, :]
    xs_ref[0:SUBLANES, :] = x[tile - SUBLANES:, :]

    sp = _softplus(-lam_ref[...])
    for n in range(blocks):
        sl = slice(n * blk, (n + 1) * blk)
        xb = xc[:, sl]
        ri = jnp.dot(xb.astype(BF16), w_ref[n], preferred_element_type=F32)
        r = _sigmoid(ri[:, :blk] + ba_ref[:, sl])
        ig = _sigmoid(ri[:, blk:] + bx_ref[:, sl])
        log_a = (-LRU_C) * r * sp[:, sl]
        a_ref[:, sl] = jnp.exp(log_a)
        th = jnp.tanh(log_a)
        u_ref[:, sl] = jnp.sqrt(-2.0 * th / (1.0 - th)) * (ig * xb)

    rowid = lax.broadcasted_iota(jnp.int32, (SUBLANES, width), 0)

    def group(g, hc):
        r0 = pl.multiple_of(g * SUBLANES, SUBLANES)
        a = a_ref[pl.ds(r0, SUBLANES), :]
        u = u_ref[pl.ds(r0, SUBLANES), :]
        for d in (1, 2, 4):
            keep = rowid >= d
            a_sh = jnp.where(keep, pltpu.roll(a, d, axis=0), 1.0)
            u_sh = jnp.where(keep, pltpu.roll(u, d, axis=0), 0.0)
            u = a * u_sh + u
            a = a * a_sh
        hv = u + a * hc
        h_ref[pl.ds(r0, SUBLANES), :] = hv
        return jnp.broadcast_to(hv[SUBLANES - 1:SUBLANES, :], (SUBLANES, width))

    hc = lax.fori_loop(0, tile // SUBLANES, group, hc_ref[...])
    hc_ref[...] = hc
    out_ref[...] = (h_ref[...] * _gelu_tanh(zy_ref[...].astype(F32))).astype(out_ref.dtype)


def _rglru(z, conv_w, conv_b, w_cat, b_a, b_x, lam, tile):
    s = z.shape[0]
    blocks, blk, _ = w_cat.shape
    width = blocks * blk
    conv_k = conv_w.shape[0]
    vec = pl.BlockSpec((1, width), lambda i: (0, 0))
    return pl.pallas_call(
        functools.partial(_rglru_kernel, blocks=blocks, conv_k=conv_k),
        grid=(s // tile,),
        in_specs=[pl.BlockSpec((tile, width), lambda i: (i, 0)),
                  pl.BlockSpec((tile, width), lambda i: (i, 1)),
                  pl.BlockSpec((conv_k, width), lambda i: (0, 0)),
                  vec,
                  pl.BlockSpec((blocks, blk, 2 * blk), lambda i: (0, 0, 0)),
                  vec, vec, vec],
        out_specs=pl.BlockSpec((tile, width), lambda i: (i, 0)),
        out_shape=jax.ShapeDtypeStruct((s, width), BF16),
        scratch_shapes=[pltpu.VMEM((tile + SUBLANES, width), F32),
                        pltpu.VMEM((tile, width), F32),
                        pltpu.VMEM((tile, width), F32),
                        pltpu.VMEM((tile, width), F32),
                        pltpu.VMEM((SUBLANES, width), F32)],
        compiler_params=_params(1),
        name="rglru",
    )(z, z, conv_w, conv_b, w_cat, b_a, b_x, lam)


def _ffn_up_kernel(h_ref, wg_ref, wu_ref, cw_ref, cb_ref, out_ref, carry_ref, *, conv_k):
    i = pl.program_id(0)
    j = pl.program_id(1)
    tm = h_ref.shape[0]
    tn = out_ref.shape[1]

    @pl.when(i == 0)
    def _():
        carry_ref[j] = jnp.zeros((SUBLANES, tn), F32)

    w = jnp.concatenate([wg_ref[...], wu_ref[...]], axis=1).astype(BF16)
    gu = jnp.dot(h_ref[...], w, preferred_element_type=F32)
    gate = gu[:, :tn]
    up = gu[:, tn:]

    def conv_of(cur, shifted):
        acc = cb_ref[...] + cw_ref[conv_k - 1:conv_k, :] * cur
        for d in range(1, conv_k):
            acc = acc + cw_ref[conv_k - 1 - d:conv_k - d, :] * shifted[d - 1]
        return acc

    conv = conv_of(gate, [pltpu.roll(gate, d, axis=0) for d in range(1, conv_k)])
    out_ref[...] = (_gelu_tanh(conv) * up).astype(out_ref.dtype)

    top = gate[:SUBLANES, :]
    tail = carry_ref[j]
    rowid = lax.broadcasted_iota(jnp.int32, (SUBLANES, tn), 0)
    top_shifted = [jnp.where(rowid < d, pltpu.roll(tail, d, axis=0), pltpu.roll(top, d, axis=0))
                   for d in range(1, conv_k)]
    out_ref[0:SUBLANES, :] = (_gelu_tanh(conv_of(top, top_shifted)) * up[:SUBLANES, :]).astype(out_ref.dtype)
    carry_ref[j] = gate[tm - SUBLANES:, :]


def _ffn_up(h, w_gate, w_up, conv_w, conv_b, layer, tm, tn):
    s, d = h.shape
    ff = w_gate.shape[2]
    conv_k = conv_w.shape[1]
    return pl.pallas_call(
        functools.partial(_ffn_up_kernel, conv_k=conv_k),
        grid=(s // tm, ff // tn),
        in_specs=[pl.BlockSpec((tm, d), lambda i, j: (i, 0)),
                  pl.BlockSpec((None, d, tn), lambda i, j: (layer, 0, j)),
                  pl.BlockSpec((None, d, tn), lambda i, j: (layer, 0, j)),
                  pl.BlockSpec((None, conv_k, tn), lambda i, j: (layer, 0, j)),
                  pl.BlockSpec((None, 1, tn), lambda i, j: (layer, 0, j))],
        out_specs=pl.BlockSpec((tm, tn), lambda i, j: (i, j)),
        out_shape=jax.ShapeDtypeStruct((s, ff), BF16),
        scratch_shapes=[pltpu.VMEM((ff // tn, SUBLANES, tn), F32)],
        compiler_params=_params(2),
        name="ffn_up",
    )(h, w_gate, w_up, conv_w, conv_b.reshape(conv_b.shape[0], 1, ff))


def _ffn_down_kernel(a_ref, w_ref, o_ref, acc_ref):
    k = pl.program_id(1)

    @pl.when(k == 0)
    def _():
        acc_ref[...] = jnp.zeros_like(acc_ref)

    acc_ref[...] += jnp.dot(a_ref[...], w_ref[...].astype(BF16), preferred_element_type=F32)

    @pl.when(k == pl.num_programs(1) - 1)
    def _():
        o_ref[...] = acc_ref[...].astype(o_ref.dtype)


def _ffn_down(hid, w_down, layer, tm, tk):
    s, ff = hid.shape
    d = w_down.shape[2]
    return pl.pallas_call(
        _ffn_down_kernel,
        grid=(s // tm, ff // tk),
        in_specs=[pl.BlockSpec((tm, tk), lambda i, k: (i, k)),
                  pl.BlockSpec((None, tk, d), lambda i, k: (layer, k, 0))],
        out_specs=pl.BlockSpec((tm, d), lambda i, k: (i, 0)),
        out_shape=jax.ShapeDtypeStruct((s, d), BF16),
        scratch_shapes=[pltpu.VMEM((tm, d), F32)],
        compiler_params=_params(2),
        name="ffn_down",
    )(hid, w_down)


def _pad_to(a, axis, size):
    pad = [(0, 0)] * a.ndim
    pad[axis] = (0, size - a.shape[axis])
    return jnp.pad(a, pad)


def kernel(x, g_mix_pre, w_in, b_i, b_f, g_mlstm_head, lru_conv_w, lru_conv_b, w_a, b_a, w_x, b_x,
           lam, w_out, g_mix_post, g_ffn_pre, w_gate, w_up, ffn_conv_w, ffn_conv_b, w_down, g_ffn_post):
    bsz, seq, d_model = x.shape
    depth = w_in.shape[0]
    heads = b_i.shape[1]
    lru_blocks, lru_blk = w_a.shape[1], w_a.shape[2]
    d_lru = lru_blocks * lru_blk
    d_mlstm = w_out.shape[1] - d_lru
    dv = d_mlstm // heads
    dk = dv // 2
    qk = heads * dk
    d_ff = w_gate.shape[2]
    gate_off = 2 * qk + 2 * d_mlstm
    assert w_in.shape[2] == gate_off + 2 * heads + 2 * d_lru
    assert d_mlstm == 2 * qk and d_lru == d_mlstm and 2 * heads <= LANES
    s = bsz * seq
    assert bsz == 1, "the recurrences carry state across row tiles of a single sequence"

    tm = min(ROW_TILE, s)
    chunk = min(MLSTM_CHUNK, s)
    lru_tile = min(LRU_TILE, s)
    norm_tm = min(256, s)
    n_main = gate_off
    assert n_main % MM_TILE_N == 0 and (2 * d_lru) % MM_TILE_N == 0 and d_model % MM_TILE_N == 0
    assert d_ff % FF_TILE == 0

    w_lru = w_in[:, :, gate_off + 2 * heads:].astype(BF16)
    w_g = _pad_to(w_in[:, :, gate_off:gate_off + 2 * heads], 2, LANES).astype(BF16)
    gate_bias = _pad_to(jnp.concatenate([b_i, b_f], axis=1), 1, LANES)
    w_cat = jnp.concatenate([w_a, w_x], axis=-1).astype(BF16)

    xf = x.reshape(s, d_model)
    row = lambda v: v.reshape(1, -1)
    h = _norm(xf, row(g_mix_pre[0]), norm_tm)
    for l in range(depth):
        zm = _matmul([(h, w_in, l, 0)], n_main, BF16, tm, MM_TILE_N, "in_proj_mlstm")
        zl = _matmul([(h, w_lru, l, 0)], 2 * d_lru, BF16, tm, MM_TILE_N, "in_proj_lru")
        gates = _matmul([(h, w_g, l, 0)], LANES, F32, tm, LANES, "gate_proj")
        hm = _mlstm(zm, gates, row(gate_bias[l]), row(g_mlstm_head[l]), heads, dk, dv, chunk)
        hl = _rglru(zl, lru_conv_w[l], row(lru_conv_b[l]), w_cat[l],
                    row(b_a[l]), row(b_x[l]), row(lam[l]), lru_tile)
        mix = _matmul([(hm, w_out, l, 0), (hl, w_out, l, 1)], d_model, BF16, tm, MM_TILE_N, "out_proj")
        xf, h = _resnorm(xf, mix, row(g_mix_post[l]), row(g_ffn_pre[l]), norm_tm)

        hid = _ffn_up(h, w_gate, w_up, ffn_conv_w, ffn_conv_b, l, tm, FF_TILE)
        ff = _ffn_down(hid, w_down, l, tm, FF_TILE)
        g_next = row(g_mix_pre[l + 1]) if l + 1 < depth else None
        xf, h = _resnorm(xf, ff, row(g_ffn_post[l]), g_next, norm_tm)
    return xf.reshape(bsz, seq, d_model)
```

```python
import functools
import math

import jax
import jax.numpy as jnp
from jax import lax
from jax.experimental import pallas as pl
from jax.experimental.pallas import tpu as pltpu

EPS = 1e-6
GATE_SOFTCAP = 15.0
LRU_C = 8.0
LANES = 128
SUBLANES = 8
VMEM_LIMIT_BYTES = 60 * 1024 * 1024
MLSTM_CHUNK = 256
LRU_TILE = 256
FF_TILE = 256
ROW_TILE = 1024
MM_TILE_N = 512

BF16 = jnp.bfloat16
F32 = jnp.float32


def _params(n_axes):
    return pltpu.CompilerParams(
        dimension_semantics=("arbitrary",) * n_axes,
        vmem_limit_bytes=VMEM_LIMIT_BYTES)


def _gelu_tanh(y):
    c = math.sqrt(2.0 / math.pi)
    return 0.5 * y * (1.0 + jnp.tanh(c * (y + 0.044715 * (y * y * y))))


def _sigmoid(y):
    return 0.5 * jnp.tanh(0.5 * y) + 0.5


def _softplus(y):
    return jnp.maximum(y, 0.0) + jnp.log1p(jnp.exp(-jnp.abs(y)))


def _rms_scale(v):
    return lax.rsqrt(jnp.mean(v * v, axis=-1, keepdims=True) + EPS)


def _norm_kernel(x_ref, g_ref, h_ref):
    x = x_ref[...]
    h_ref[...] = (x * _rms_scale(x) * g_ref[...]).astype(h_ref.dtype)


def _norm(x, g, tm):
    s, d = x.shape
    return pl.pallas_call(
        _norm_kernel,
        grid=(s // tm,),
        in_specs=[pl.BlockSpec((tm, d), lambda i: (i, 0)),
                  pl.BlockSpec((1, d), lambda i: (0, 0))],
        out_specs=pl.BlockSpec((tm, d), lambda i: (i, 0)),
        out_shape=jax.ShapeDtypeStruct((s, d), BF16),
        compiler_params=_params(1),
        name="rmsnorm",
    )(x, g)


def _resnorm_kernel(x_ref, y_ref, gp_ref, gn_ref, xo_ref, h_ref):
    y = y_ref[...].astype(F32)
    xn = x_ref[...] + y * _rms_scale(y) * gp_ref[...]
    xo_ref[...] = xn
    h_ref[...] = (xn * _rms_scale(xn) * gn_ref[...]).astype(h_ref.dtype)


def _resnorm_last_kernel(x_ref, y_ref, gp_ref, xo_ref):
    y = y_ref[...].astype(F32)
    xo_ref[...] = x_ref[...] + y * _rms_scale(y) * gp_ref[...]


def _resnorm(x, y, g_post, g_next, tm):
    s, d = x.shape
    row = pl.BlockSpec((tm, d), lambda i: (i, 0))
    vec = pl.BlockSpec((1, d), lambda i: (0, 0))
    if g_next is None:
        return pl.pallas_call(
            _resnorm_last_kernel, grid=(s // tm,),
            in_specs=[row, row, vec], out_specs=row,
            out_shape=jax.ShapeDtypeStruct((s, d), F32),
            compiler_params=_params(1), name="resnorm_last",
        )(x, y, g_post), None
    return pl.pallas_call(
        _resnorm_kernel, grid=(s // tm,),
        in_specs=[row, row, vec, vec], out_specs=[row, row],
        out_shape=[jax.ShapeDtypeStruct((s, d), F32), jax.ShapeDtypeStruct((s, d), BF16)],
        compiler_params=_params(1), name="resnorm",
    )(x, y, g_post, g_next)


def _mm_kernel(*refs, n_pairs):
    o_ref = refs[-1]
    acc = None
    for p in range(n_pairs):
        d = jnp.dot(refs[2 * p][...], refs[2 * p + 1][...].astype(BF16), preferred_element_type=F32)
        acc = d if acc is None else acc + d
    o_ref[...] = acc.astype(o_ref.dtype)


def _matmul(pairs, n_out, out_dtype, tm, tn, name):
    s = pairs[0][0].shape[0]
    in_specs, args = [], []
    for a, w, layer, kblk, jblk0 in pairs:
        k = a.shape[1]
        in_specs += [pl.BlockSpec((tm, k), lambda i, j: (i, 0)),
                     pl.BlockSpec((None, k, tn), lambda i, j, layer=layer, kblk=kblk, jblk0=jblk0:
                                  (layer, kblk, jblk0 + j))]
        args += [a, w]
    return pl.pallas_call(
        functools.partial(_mm_kernel, n_pairs=len(pairs)),
        grid=(s // tm, n_out // tn),
        in_specs=in_specs,
        out_specs=pl.BlockSpec((tm, tn), lambda i, j: (i, j)),
        out_shape=jax.ShapeDtypeStruct((s, n_out), out_dtype),
        compiler_params=_params(2),
        name=name,
    )(*args)


def _mm_nt_kernel(a_ref, w_ref, o_ref):
    o_ref[...] = lax.dot_general(a_ref[...], w_ref[0].astype(BF16), (((1,), (1,)), ((), ())),
                                 preferred_element_type=F32).astype(o_ref.dtype)


def _matmul_nt(a, w_t, layer, row0, n_out, out_dtype, tm, tn, name):
    s, k = a.shape
    assert row0 % SUBLANES == 0 and n_out % tn == 0
    return pl.pallas_call(
        _mm_nt_kernel,
        grid=(s // tm, n_out // tn),
        in_specs=[pl.BlockSpec((tm, k), lambda i, j: (i, 0)),
                  pl.BlockSpec((pl.Element(1), pl.Element(tn), pl.Element(k)),
                               lambda i, j: (layer, pl.multiple_of(row0 + j * tn, SUBLANES), 0))],
        out_specs=pl.BlockSpec((tm, tn), lambda i, j: (i, j)),
        out_shape=jax.ShapeDtypeStruct((s, n_out), out_dtype),
        compiler_params=_params(2),
        name=name,
    )(a, w_t)


def _mlstm_kernel(q_ref, k_ref, v_ref, o_ref, gt_ref, gb_ref, gh_ref, out_ref,
                  c_ref, n_ref, m_ref, *, heads, dk, dv):
    chunk = q_ref.shape[0]
    scale = dk ** -0.5

    @pl.when(pl.program_id(0) == 0)
    def _():
        c_ref[...] = jnp.zeros_like(c_ref)
        n_ref[...] = jnp.zeros_like(n_ref)
        m_ref[...] = jnp.zeros_like(m_ref)

    pre = gt_ref[...] + gb_ref[...]
    capped = GATE_SOFTCAP * jnp.tanh(pre * (1.0 / GATE_SOFTCAP))
    logf = -_softplus(-capped)
    rows = lax.broadcasted_iota(jnp.int32, (chunk, chunk), 0)
    cols = lax.broadcasted_iota(jnp.int32, (chunk, chunk), 1)
    causal = rows >= cols
    tri = jnp.where(causal, 1.0, 0.0).astype(BF16)
    p0 = logf.astype(BF16)
    r1 = logf - p0.astype(F32)
    p1 = r1.astype(BF16)
    p2 = (r1 - p1.astype(F32)).astype(BF16)
    bcum = (jnp.dot(tri, p0, preferred_element_type=F32)
            + jnp.dot(tri, p1, preferred_element_type=F32)
            + jnp.dot(tri, p2, preferred_element_type=F32))
    wmat = capped - pltpu.roll(bcum, LANES - heads, axis=1)
    wmat_t = wmat.T

    for h in range(heads):
        q = q_ref[:, h * dk:(h + 1) * dk]
        k = k_ref[:, h * dk:(h + 1) * dk]
        v = v_ref[:, h * dv:(h + 1) * dv]
        b_col = bcum[:, heads + h:heads + h + 1]
        w_col = wmat[:, h:h + 1]
        w_row = wmat_t[h:h + 1, :]
        m_prev = m_ref[h, :, 0:1]
        c_prev = c_ref[h]
        n_prev = n_ref[h]

        log_d = jnp.where(causal, b_col + w_row, -jnp.inf)
        m_inter = b_col + m_prev
        m_t = jnp.maximum(m_inter, jnp.max(log_d, axis=1, keepdims=True))
        dmat = jnp.exp(log_d - m_t)
        s_qk = lax.dot_general(q, k, (((1,), (1,)), ((), ())), preferred_element_type=F32)
        scores = s_qk * dmat * scale
        inter = jnp.exp(m_inter - m_t) * scale
        num = inter * jnp.dot(q, c_prev.astype(BF16), preferred_element_type=F32) \
            + jnp.dot(scores.astype(BF16), v, preferred_element_type=F32)
        den = inter * jnp.sum(q.astype(F32) * n_prev, axis=1, keepdims=True) \
            + jnp.sum(scores, axis=1, keepdims=True)
        hval = num * (1.0 / jnp.maximum(jnp.abs(den), jnp.exp(-m_t)))
        hval = hval * _rms_scale(hval) * gh_ref[:, h * dv:(h + 1) * dv]
        gate = _sigmoid(o_ref[:, h * dv:(h + 1) * dv].astype(F32))
        out_ref[:, h * dv:(h + 1) * dv] = (hval * gate).astype(out_ref.dtype)

        b_last = b_col[chunk - 1:chunk, :]
        m_new = jnp.maximum(b_last + m_prev,
                            b_last + jnp.max(w_col, axis=0, keepdims=True))
        decay = jnp.exp(b_last + m_prev - m_new)
        wk = jnp.exp(b_last + w_col - m_new)
        kw = k.astype(F32) * wk
        c_ref[h] = decay * c_prev + lax.dot_general(
            kw.astype(BF16), v, (((0,), (0,)), ((), ())), preferred_element_type=F32)
        n_ref[h] = decay * n_prev + jnp.sum(kw, axis=0, keepdims=True)
        m_ref[h] = jnp.broadcast_to(m_new, (1, LANES))


def _mlstm(z, gates, gate_bias, g_head, heads, dk, dv, chunk):
    s = z.shape[0]
    qk = heads * dk
    dm = heads * dv
    return pl.pallas_call(
        functools.partial(_mlstm_kernel, heads=heads, dk=dk, dv=dv),
        grid=(s // chunk,),
        in_specs=[pl.BlockSpec((chunk, qk), lambda c: (c, 0)),
                  pl.BlockSpec((chunk, qk), lambda c: (c, 1)),
                  pl.BlockSpec((chunk, dm), lambda c: (c, 1)),
                  pl.BlockSpec((chunk, dm), lambda c: (c, 2)),
                  pl.BlockSpec((chunk, LANES), lambda c: (c, 0)),
                  pl.BlockSpec((1, LANES), lambda c: (0, 0)),
                  pl.BlockSpec((1, dm), lambda c: (0, 0))],
        out_specs=pl.BlockSpec((chunk, dm), lambda c: (c, 0)),
        out_shape=jax.ShapeDtypeStruct((s, dm), BF16),
        scratch_shapes=[pltpu.VMEM((heads, dk, dv), F32),
                        pltpu.VMEM((heads, 1, dk), F32),
                        pltpu.VMEM((heads, 1, LANES), F32)],
        compiler_params=_params(1),
        name="mlstm",
    )(z, z, z, z, gates, gate_bias, g_head)


def _rglru_kernel(zx_ref, zy_ref, cw_ref, cb_ref, w_ref, ba_ref, bx_ref, lam_ref, out_ref,
                  xs_ref, a_ref, u_ref, h_ref, hc_ref, *, blocks, conv_k):
    tile, width = zx_ref.shape
    blk = width // blocks

    @pl.when(pl.program_id(0) == 0)
    def _():
        xs_ref[0:SUBLANES, :] = jnp.zeros((SUBLANES, width), F32)
        hc_ref[...] = jnp.zeros_like(hc_ref)

    x = zx_ref[...].astype(F32)
    xs_ref[SUBLANES:SUBLANES + tile, :] = x
    xc = cb_ref[...] + cw_ref[conv_k - 1:conv_k, :] * x
    for j in range(conv_k - 1):
        off = SUBLANES - (conv_k - 1) + j
        xc = xc + cw_ref[j:j + 1, :] * xs_ref[off:off + tile, :]
    xs_ref[0:SUBLANES, :] = x[tile - SUBLANES:, :]

    sp = _softplus(-lam_ref[...])
    for n in range(blocks):
        sl = slice(n * blk, (n + 1) * blk)
        xb = xc[:, sl]
        ri = jnp.dot(xb.astype(BF16), w_ref[n], preferred_element_type=F32)
        r = _sigmoid(ri[:, :blk] + ba_ref[:, sl])
        ig = _sigmoid(ri[:, blk:] + bx_ref[:, sl])
        log_a = (-LRU_C) * r * sp[:, sl]
        a_ref[:, sl] = jnp.exp(log_a)
        th = jnp.tanh(log_a)
        u_ref[:, sl] = jnp.sqrt(-2.0 * th / (1.0 - th)) * (ig * xb)

    rowid = lax.broadcasted_iota(jnp.int32, (SUBLANES, width), 0)

    def group(g, hc):
        r0 = pl.multiple_of(g * SUBLANES, SUBLANES)
        a = a_ref[pl.ds(r0, SUBLANES), :]
        u = u_ref[pl.ds(r0, SUBLANES), :]
        for d in (1, 2, 4):
            keep = rowid >= d
            a_sh = jnp.where(keep, pltpu.roll(a, d, axis=0), 1.0)
            u_sh = jnp.where(keep, pltpu.roll(u, d, axis=0), 0.0)
            u = a * u_sh + u
            a = a * a_sh
        hv = u + a * hc
        h_ref[pl.ds(r0, SUBLANES), :] = hv
        return jnp.broadcast_to(hv[SUBLANES - 1:SUBLANES, :], (SUBLANES, width))

    hc = lax.fori_loop(0, tile // SUBLANES, group, hc_ref[...])
    hc_ref[...] = hc
    out_ref[...] = (h_ref[...] * _gelu_tanh(zy_ref[...].astype(F32))).astype(out_ref.dtype)


def _rglru(z, conv_w, conv_b, w_cat, b_a, b_x, lam, tile):
    s = z.shape[0]
    blocks, blk, _ = w_cat.shape
    width = blocks * blk
    conv_k = conv_w.shape[0]
    vec = pl.BlockSpec((1, width), lambda i: (0, 0))
    return pl.pallas_call(
        functools.partial(_rglru_kernel, blocks=blocks, conv_k=conv_k),
        grid=(s // tile,),
        in_specs=[pl.BlockSpec((tile, width), lambda i: (i, 0)),
                  pl.BlockSpec((tile, width), lambda i: (i, 1)),
                  pl.BlockSpec((conv_k, width), lambda i: (0, 0)),
                  vec,
                  pl.BlockSpec((blocks, blk, 2 * blk), lambda i: (0, 0, 0)),
                  vec, vec, vec],
        out_specs=pl.BlockSpec((tile, width), lambda i: (i, 0)),
        out_shape=jax.ShapeDtypeStruct((s, width), BF16),
        scratch_shapes=[pltpu.VMEM((tile + SUBLANES, width), F32),
                        pltpu.VMEM((tile, width), F32),
                        pltpu.VMEM((tile, width), F32),
                        pltpu.VMEM((tile, width), F32),
                        pltpu.VMEM((SUBLANES, width), F32)],
        compiler_params=_params(1),
        name="rglru",
    )(z, z, conv_w, conv_b, w_cat, b_a, b_x, lam)


def _ffn_up_kernel(h_ref, wg_ref, wu_ref, cw_ref, cb_ref, out_ref, carry_ref, *, conv_k):
    i = pl.program_id(0)
    j = pl.program_id(1)
    tm = h_ref.shape[0]
    tn = out_ref.shape[1]

    @pl.when(i == 0)
    def _():
        carry_ref[j] = jnp.zeros((SUBLANES, tn), F32)

    w = jnp.concatenate([wg_ref[...], wu_ref[...]], axis=1).astype(BF16)
    gu = jnp.dot(h_ref[...], w, preferred_element_type=F32)
    gate = gu[:, :tn]
    up = gu[:, tn:]

    def conv_of(cur, shifted):
        acc = cb_ref[...] + cw_ref[conv_k - 1:conv_k, :] * cur
        for d in range(1, conv_k):
            acc = acc + cw_ref[conv_k - 1 - d:conv_k - d, :] * shifted[d - 1]
        return acc

    conv = conv_of(gate, [pltpu.roll(gate, d, axis=0) for d in range(1, conv_k)])
    out_ref[...] = (_gelu_tanh(conv) * up).astype(out_ref.dtype)

    top = gate[:SUBLANES, :]
    tail = carry_ref[j]
    rowid = lax.broadcasted_iota(jnp.int32, (SUBLANES, tn), 0)
    top_shifted = [jnp.where(rowid < d, pltpu.roll(tail, d, axis=0), pltpu.roll(top, d, axis=0))
                   for d in range(1, conv_k)]
    out_ref[0:SUBLANES, :] = (_gelu_tanh(conv_of(top, top_shifted)) * up[:SUBLANES, :]).astype(out_ref.dtype)
    carry_ref[j] = gate[tm - SUBLANES:, :]


def _ffn_up(h, w_gate, w_up, conv_w, conv_b, layer, tm, tn):
    s, d = h.shape
    ff = w_gate.shape[2]
    conv_k = conv_w.shape[1]
    return pl.pallas_call(
        functools.partial(_ffn_up_kernel, conv_k=conv_k),
        grid=(s // tm, ff // tn),
        in_specs=[pl.BlockSpec((tm, d), lambda i, j: (i, 0)),
                  pl.BlockSpec((None, d, tn), lambda i, j: (layer, 0, j)),
                  pl.BlockSpec((None, d, tn), lambda i, j: (layer, 0, j)),
                  pl.BlockSpec((None, conv_k, tn), lambda i, j: (layer, 0, j)),
                  pl.BlockSpec((None, 1, tn), lambda i, j: (layer, 0, j))],
        out_specs=pl.BlockSpec((tm, tn), lambda i, j: (i, j)),
        out_shape=jax.ShapeDtypeStruct((s, ff), BF16),
        scratch_shapes=[pltpu.VMEM((ff // tn, SUBLANES, tn), F32)],
        compiler_params=_params(2),
        name="ffn_up",
    )(h, w_gate, w_up, conv_w, conv_b.reshape(conv_b.shape[0], 1, ff))


def _ffn_down_kernel(a_ref, w_ref, ar_ref, wr_ref, o_ref, acc_ref):
    k = pl.program_id(1)

    @pl.when(k == 0)
    def _():
        acc_ref[...] = jnp.dot(ar_ref[...], wr_ref[...].astype(BF16), preferred_element_type=F32)

    acc_ref[...] += jnp.dot(a_ref[...], w_ref[...].astype(BF16), preferred_element_type=F32)

    @pl.when(k == pl.num_programs(1) - 1)
    def _():
        o_ref[...] = acc_ref[...].astype(o_ref.dtype)


def _ffn_down(hid, w_down, layer, tm, tk, rem):
    s, ff = hid.shape
    d = w_down.shape[2]
    assert 0 < rem < tk and (ff - rem) % tk == 0 and ff % rem == 0
    rem_blk = ff // rem - 1
    once = pl.Buffered(1)
    return pl.pallas_call(
        _ffn_down_kernel,
        grid=(s // tm, (ff - rem) // tk),
        in_specs=[pl.BlockSpec((tm, tk), lambda i, k: (i, k)),
                  pl.BlockSpec((None, tk, d), lambda i, k: (layer, k, 0)),
                  pl.BlockSpec((tm, rem), lambda i, k: (i, rem_blk), pipeline_mode=once),
                  pl.BlockSpec((None, rem, d), lambda i, k: (layer, rem_blk, 0), pipeline_mode=once)],
        out_specs=pl.BlockSpec((tm, d), lambda i, k: (i, 0), pipeline_mode=once),
        out_shape=jax.ShapeDtypeStruct((s, d), BF16),
        scratch_shapes=[pltpu.VMEM((tm, d), F32)],
        compiler_params=_params(2),
        name="ffn_down",
    )(hid, w_down, hid, w_down)


def _pad_to(a, axis, size):
    pad = [(0, 0)] * a.ndim
    pad[axis] = (0, size - a.shape[axis])
    return jnp.pad(a, pad)


def kernel(x, g_mix_pre, w_in, b_i, b_f, g_mlstm_head, lru_conv_w, lru_conv_b, w_a, b_a, w_x, b_x,
           lam, w_out, g_mix_post, g_ffn_pre, w_gate, w_up, ffn_conv_w, ffn_conv_b, w_down, g_ffn_post):
    bsz, seq, d_model = x.shape
    depth = w_in.shape[0]
    heads = b_i.shape[1]
    lru_blocks, lru_blk = w_a.shape[1], w_a.shape[2]
    d_lru = lru_blocks * lru_blk
    d_mlstm = w_out.shape[1] - d_lru
    dv = d_mlstm // heads
    dk = dv // 2
    qk = heads * dk
    d_ff = w_gate.shape[2]
    gate_off = 2 * qk + 2 * d_mlstm
    assert w_in.shape[2] == gate_off + 2 * heads + 2 * d_lru
    assert d_mlstm == 2 * qk and d_lru == d_mlstm and 2 * heads <= LANES
    s = bsz * seq
    assert bsz == 1, "the recurrences carry state across row tiles of a single sequence"

    tm = min(ROW_TILE, s)
    tm_big = min(2 * ROW_TILE, s)
    chunk = min(MLSTM_CHUNK, s)
    lru_tile = min(LRU_TILE, s)
    norm_tm = min(256, s)
    n_main = gate_off
    assert n_main % MM_TILE_N == 0 and (2 * d_lru) % MM_TILE_N == 0 and d_model % MM_TILE_N == 0
    assert d_ff % FF_TILE == 0 and d_ff % (2 * FF_TILE) == FF_TILE and gate_off % LANES == 0

    w_in_t = jnp.swapaxes(w_in, 1, 2)
    gate_bias = _pad_to(jnp.concatenate([b_i, b_f], axis=1), 1, LANES)
    w_cat = jnp.concatenate([w_a, w_x], axis=-1).astype(BF16)

    xf = x.reshape(s, d_model)
    row = lambda v: v.reshape(1, -1)
    h = _norm(xf, row(g_mix_pre[0]), norm_tm)
    for l in range(depth):
        zm = _matmul_nt(h, w_in_t, l, 0, n_main, BF16, tm_big, MM_TILE_N, "in_proj_mlstm")
        gates = _matmul_nt(h, w_in_t, l, gate_off, LANES, F32, tm, LANES, "gate_proj")
        zl = _matmul_nt(h, w_in_t, l, gate_off + 2 * heads, 2 * d_lru, BF16, tm_big, MM_TILE_N, "in_proj_lru")
        hm = _mlstm(zm, gates, row(gate_bias[l]), row(g_mlstm_head[l]), heads, dk, dv, chunk)
        hl = _rglru(zl, lru_conv_w[l], row(lru_conv_b[l]), w_cat[l],
                    row(b_a[l]), row(b_x[l]), row(lam[l]), lru_tile)
        mix = _matmul([(hm, w_out, l, 0, 0), (hl, w_out, l, 1, 0)], d_model, BF16, tm_big, MM_TILE_N,
                      "out_proj")
        xf, h = _resnorm(xf, mix, row(g_mix_post[l]), row(g_ffn_pre[l]), norm_tm)

        hid = _ffn_up(h, w_gate, w_up, ffn_conv_w, ffn_conv_b, l, tm_big, FF_TILE)
        ff = _ffn_down(hid, w_down, l, tm, 2 * FF_TILE, FF_TILE)
        g_next = row(g_mix_pre[l + 1]) if l + 1 < depth else None
        xf, h = _resnorm(xf, ff, row(g_ffn_post[l]), g_next, norm_tm)
    return xf.reshape(bsz, seq, d_model)
```

```python
import functools
import math

import jax
import jax.numpy as jnp
from jax import lax
from jax.experimental import pallas as pl
from jax.experimental.pallas import tpu as pltpu

EPS = 1e-6
GATE_SOFTCAP = 15.0
LRU_C = 8.0
LANES = 128
SUBLANES = 8
VMEM_LIMIT_BYTES = 60 * 1024 * 1024
MLSTM_CHUNK = 256
LRU_TILE = 256
FF_TILE = 256
ROW_TILE = 1024
MM_TILE_N = 512
FFN_ROW_CHUNK = 256

BF16 = jnp.bfloat16
F32 = jnp.float32


def _params(n_axes):
    return pltpu.CompilerParams(
        dimension_semantics=("arbitrary",) * n_axes,
        vmem_limit_bytes=VMEM_LIMIT_BYTES)


def _gelu_tanh(y):
    c = math.sqrt(2.0 / math.pi)
    return 0.5 * y * (1.0 + jnp.tanh(c * (y + 0.044715 * (y * y * y))))


def _sigmoid(y):
    return 0.5 * jnp.tanh(0.5 * y) + 0.5


def _softplus(y):
    return jnp.maximum(y, 0.0) + jnp.log1p(jnp.exp(-jnp.abs(y)))


def _rms_scale(v):
    return lax.rsqrt(jnp.mean(v * v, axis=-1, keepdims=True) + EPS)


def _norm_kernel(x_ref, g_ref, h_ref):
    x = x_ref[...]
    h_ref[...] = (x * _rms_scale(x) * g_ref[...]).astype(h_ref.dtype)


def _norm(x, g, tm):
    s, d = x.shape
    return pl.pallas_call(
        _norm_kernel,
        grid=(s // tm,),
        in_specs=[pl.BlockSpec((tm, d), lambda i: (i, 0)),
                  pl.BlockSpec((1, d), lambda i: (0, 0))],
        out_specs=pl.BlockSpec((tm, d), lambda i: (i, 0)),
        out_shape=jax.ShapeDtypeStruct((s, d), BF16),
        compiler_params=_params(1),
        name="rmsnorm",
    )(x, g)


def _resnorm_kernel(x_ref, y_ref, gp_ref, gn_ref, xo_ref, h_ref):
    y = y_ref[...].astype(F32)
    xn = x_ref[...] + y * _rms_scale(y) * gp_ref[...]
    xo_ref[...] = xn
    h_ref[...] = (xn * _rms_scale(xn) * gn_ref[...]).astype(h_ref.dtype)


def _resnorm_last_kernel(x_ref, y_ref, gp_ref, xo_ref):
    y = y_ref[...].astype(F32)
    xo_ref[...] = x_ref[...] + y * _rms_scale(y) * gp_ref[...]


def _resnorm(x, y, g_post, g_next, tm):
    s, d = x.shape
    row = pl.BlockSpec((tm, d), lambda i: (i, 0))
    vec = pl.BlockSpec((1, d), lambda i: (0, 0))
    if g_next is None:
        return pl.pallas_call(
            _resnorm_last_kernel, grid=(s // tm,),
            in_specs=[row, row, vec], out_specs=row,
            out_shape=jax.ShapeDtypeStruct((s, d), F32),
            compiler_params=_params(1), name="resnorm_last",
        )(x, y, g_post), None
    return pl.pallas_call(
        _resnorm_kernel, grid=(s // tm,),
        in_specs=[row, row, vec, vec], out_specs=[row, row],
        out_shape=[jax.ShapeDtypeStruct((s, d), F32), jax.ShapeDtypeStruct((s, d), BF16)],
        compiler_params=_params(1), name="resnorm",
    )(x, y, g_post, g_next)


def _mm_kernel(*refs, n_pairs):
    o_ref = refs[-1]
    acc = None
    for p in range(n_pairs):
        d = jnp.dot(refs[2 * p][...], refs[2 * p + 1][...].astype(BF16), preferred_element_type=F32)
        acc = d if acc is None else acc + d
    o_ref[...] = acc.astype(o_ref.dtype)


def _matmul(pairs, n_out, out_dtype, tm, tn, name):
    s = pairs[0][0].shape[0]
    in_specs, args = [], []
    for a, w, layer, kblk, jblk0 in pairs:
        k = a.shape[1]
        in_specs += [pl.BlockSpec((tm, k), lambda i, j: (i, 0)),
                     pl.BlockSpec((None, k, tn), lambda i, j, layer=layer, kblk=kblk, jblk0=jblk0:
                                  (layer, kblk, jblk0 + j))]
        args += [a, w]
    return pl.pallas_call(
        functools.partial(_mm_kernel, n_pairs=len(pairs)),
        grid=(s // tm, n_out // tn),
        in_specs=in_specs,
        out_specs=pl.BlockSpec((tm, tn), lambda i, j: (i, j)),
        out_shape=jax.ShapeDtypeStruct((s, n_out), out_dtype),
        compiler_params=_params(2),
        name=name,
    )(*args)


def _mm_nt_kernel(a_ref, w_ref, o_ref):
    o_ref[...] = lax.dot_general(a_ref[...], w_ref[0].astype(BF16), (((1,), (1,)), ((), ())),
                                 preferred_element_type=F32).astype(o_ref.dtype)


def _matmul_nt(a, w_t, layer, row0, n_out, out_dtype, tm, tn, name):
    s, k = a.shape
    assert row0 % SUBLANES == 0 and n_out % tn == 0
    return pl.pallas_call(
        _mm_nt_kernel,
        grid=(s // tm, n_out // tn),
        in_specs=[pl.BlockSpec((tm, k), lambda i, j: (i, 0)),
                  pl.BlockSpec((pl.Element(1), pl.Element(tn), pl.Element(k)),
                               lambda i, j: (layer, pl.multiple_of(row0 + j * tn, SUBLANES), 0))],
        out_specs=pl.BlockSpec((tm, tn), lambda i, j: (i, j)),
        out_shape=jax.ShapeDtypeStruct((s, n_out), out_dtype),
        compiler_params=_params(2),
        name=name,
    )(a, w_t)


def _mlstm_kernel(q_ref, k_ref, v_ref, o_ref, gt_ref, gb_ref, gh_ref, out_ref,
                  c_ref, n_ref, m_ref, *, heads, dk, dv):
    chunk = q_ref.shape[0]
    scale = dk ** -0.5

    @pl.when(pl.program_id(0) == 0)
    def _():
        c_ref[...] = jnp.zeros_like(c_ref)
        n_ref[...] = jnp.zeros_like(n_ref)
        m_ref[...] = jnp.zeros_like(m_ref)

    pre = gt_ref[...] + gb_ref[...]
    capped = GATE_SOFTCAP * jnp.tanh(pre * (1.0 / GATE_SOFTCAP))
    logf = -_softplus(-capped)
    rows = lax.broadcasted_iota(jnp.int32, (chunk, chunk), 0)
    cols = lax.broadcasted_iota(jnp.int32, (chunk, chunk), 1)
    causal = rows >= cols
    tri = jnp.where(causal, 1.0, 0.0).astype(BF16)
    p0 = logf.astype(BF16)
    r1 = logf - p0.astype(F32)
    p1 = r1.astype(BF16)
    p2 = (r1 - p1.astype(F32)).astype(BF16)
    bcum = (jnp.dot(tri, p0, preferred_element_type=F32)
            + jnp.dot(tri, p1, preferred_element_type=F32)
            + jnp.dot(tri, p2, preferred_element_type=F32))
    wmat = capped - pltpu.roll(bcum, LANES - heads, axis=1)
    wmat_t = wmat.T

    for h in range(heads):
        q = q_ref[:, h * dk:(h + 1) * dk]
        k = k_ref[:, h * dk:(h + 1) * dk]
        v = v_ref[:, h * dv:(h + 1) * dv]
        b_col = bcum[:, heads + h:heads + h + 1]
        w_col = wmat[:, h:h + 1]
        w_row = wmat_t[h:h + 1, :]
        m_prev = m_ref[h, :, 0:1]
        c_prev = c_ref[h]
        n_prev = n_ref[h]

        log_d = jnp.where(causal, b_col + w_row, -jnp.inf)
        m_inter = b_col + m_prev
        m_t = jnp.maximum(m_inter, jnp.max(log_d, axis=1, keepdims=True))
        dmat = jnp.exp(log_d - m_t)
        s_qk = lax.dot_general(q, k, (((1,), (1,)), ((), ())), preferred_element_type=F32)
        scores = s_qk * dmat * scale
        inter = jnp.exp(m_inter - m_t) * scale
        num = inter * jnp.dot(q, c_prev.astype(BF16), preferred_element_type=F32) \
            + jnp.dot(scores.astype(BF16), v, preferred_element_type=F32)
        den = inter * jnp.sum(q.astype(F32) * n_prev, axis=1, keepdims=True) \
            + jnp.sum(scores, axis=1, keepdims=True)
        hval = num * (1.0 / jnp.maximum(jnp.abs(den), jnp.exp(-m_t)))
        hval = hval * _rms_scale(hval) * gh_ref[:, h * dv:(h + 1) * dv]
        gate = _sigmoid(o_ref[:, h * dv:(h + 1) * dv].astype(F32))
        out_ref[:, h * dv:(h + 1) * dv] = (hval * gate).astype(out_ref.dtype)

        b_last = b_col[chunk - 1:chunk, :]
        m_new = jnp.maximum(b_last + m_prev,
                            b_last + jnp.max(w_col, axis=0, keepdims=True))
        decay = jnp.exp(b_last + m_prev - m_new)
        wk = jnp.exp(b_last + w_col - m_new)
        kw = k.astype(F32) * wk
        c_ref[h] = decay * c_prev + lax.dot_general(
            kw.astype(BF16), v, (((0,), (0,)), ((), ())), preferred_element_type=F32)
        n_ref[h] = decay * n_prev + jnp.sum(kw, axis=0, keepdims=True)
        m_ref[h] = jnp.broadcast_to(m_new, (1, LANES))


def _mlstm(z, gates, gate_bias, g_head, heads, dk, dv, chunk):
    s = z.shape[0]
    qk = heads * dk
    dm = heads * dv
    return pl.pallas_call(
        functools.partial(_mlstm_kernel, heads=heads, dk=dk, dv=dv),
        grid=(s // chunk,),
        in_specs=[pl.BlockSpec((chunk, qk), lambda c: (c, 0)),
                  pl.BlockSpec((chunk, qk), lambda c: (c, 1)),
                  pl.BlockSpec((chunk, dm), lambda c: (c, 1)),
                  pl.BlockSpec((chunk, dm), lambda c: (c, 2)),
                  pl.BlockSpec((chunk, LANES), lambda c: (c, 0)),
                  pl.BlockSpec((1, LANES), lambda c: (0, 0)),
                  pl.BlockSpec((1, dm), lambda c: (0, 0))],
        out_specs=pl.BlockSpec((chunk, dm), lambda c: (c, 0)),
        out_shape=jax.ShapeDtypeStruct((s, dm), BF16),
        scratch_shapes=[pltpu.VMEM((heads, dk, dv), F32),
                        pltpu.VMEM((heads, 1, dk), F32),
                        pltpu.VMEM((heads, 1, LANES), F32)],
        compiler_params=_params(1),
        name="mlstm",
    )(z, z, z, z, gates, gate_bias, g_head)


def _rglru_kernel(zx_ref, zy_ref, cw_ref, cb_ref, w_ref, ba_ref, bx_ref, lam_ref, out_ref,
                  xs_ref, a_ref, u_ref, h_ref, hc_ref, *, blocks, conv_k):
    tile, width = zx_ref.shape
    blk = width // blocks

    @pl.when(pl.program_id(0) == 0)
    def _():
        xs_ref[0:SUBLANES, :] = jnp.zeros((SUBLANES, width), F32)
        hc_ref[...] = jnp.zeros_like(hc_ref)

    x = zx_ref[...].astype(F32)
    xs_ref[SUBLANES:SUBLANES + tile, :] = x
    xc = cb_ref[...] + cw_ref[conv_k - 1:conv_k, :] * x
    for j in range(conv_k - 1):
        off = SUBLANES - (conv_k - 1) + j
        xc = xc + cw_ref[j:j + 1, :] * xs_ref[off:off + tile, :]
    xs_ref[0:SUBLANES, :] = x[tile - SUBLANES:, :]

    sp = _softplus(-lam_ref[...])
    for n in range(blocks):
        sl = slice(n * blk, (n + 1) * blk)
        xb = xc[:, sl]
        ri = jnp.dot(xb.astype(BF16), w_ref[n], preferred_element_type=F32)
        r = _sigmoid(ri[:, :blk] + ba_ref[:, sl])
        ig = _sigmoid(ri[:, blk:] + bx_ref[:, sl])
        log_a = (-LRU_C) * r * sp[:, sl]
        a_ref[:, sl] = jnp.exp(log_a)
        th = jnp.tanh(log_a)
        u_ref[:, sl] = jnp.sqrt(-2.0 * th / (1.0 - th)) * (ig * xb)

    rowid = lax.broadcasted_iota(jnp.int32, (SUBLANES, width), 0)

    def group(g, hc):
        r0 = pl.multiple_of(g * SUBLANES, SUBLANES)
        a = a_ref[pl.ds(r0, SUBLANES), :]
        u = u_ref[pl.ds(r0, SUBLANES), :]
        for d in (1, 2, 4):
            keep = rowid >= d
            a_sh = jnp.where(keep, pltpu.roll(a, d, axis=0), 1.0)
            u_sh = jnp.where(keep, pltpu.roll(u, d, axis=0), 0.0)
            u = a * u_sh + u
            a = a * a_sh
        hv = u + a * hc
        h_ref[pl.ds(r0, SUBLANES), :] = hv
        return jnp.broadcast_to(hv[SUBLANES - 1:SUBLANES, :], (SUBLANES, width))

    hc = lax.fori_loop(0, tile // SUBLANES, group, hc_ref[...])
    hc_ref[...] = hc
    out_ref[...] = (h_ref[...] * _gelu_tanh(zy_ref[...].astype(F32))).astype(out_ref.dtype)


def _rglru(z, conv_w, conv_b, w_cat, b_a, b_x, lam, tile):
    s = z.shape[0]
    blocks, blk, _ = w_cat.shape
    width = blocks * blk
    conv_k = conv_w.shape[0]
    vec = pl.BlockSpec((1, width), lambda i: (0, 0))
    return pl.pallas_call(
        functools.partial(_rglru_kernel, blocks=blocks, conv_k=conv_k),
        grid=(s // tile,),
        in_specs=[pl.BlockSpec((tile, width), lambda i: (i, 0)),
                  pl.BlockSpec((tile, width), lambda i: (i, 1)),
                  pl.BlockSpec((conv_k, width), lambda i: (0, 0)),
                  vec,
                  pl.BlockSpec((blocks, blk, 2 * blk), lambda i: (0, 0, 0)),
                  vec, vec, vec],
        out_specs=pl.BlockSpec((tile, width), lambda i: (i, 0)),
        out_shape=jax.ShapeDtypeStruct((s, width), BF16),
        scratch_shapes=[pltpu.VMEM((tile + SUBLANES, width), F32),
                        pltpu.VMEM((tile, width), F32),
                        pltpu.VMEM((tile, width), F32),
                        pltpu.VMEM((tile, width), F32),
                        pltpu.VMEM((SUBLANES, width), F32)],
        compiler_params=_params(1),
        name="rglru",
    )(z, z, conv_w, conv_b, w_cat, b_a, b_x, lam)


def _ffn_up_kernel(h_ref, wg_ref, wu_ref, cw_ref, cb_ref, out_ref, carry_ref, *, conv_k):
    i = pl.program_id(0)
    j = pl.program_id(1)
    tm = h_ref.shape[0]
    tn = out_ref.shape[1]

    @pl.when(i == 0)
    def _():
        carry_ref[j] = jnp.zeros((SUBLANES, tn), F32)

    w = jnp.concatenate([wg_ref[...], wu_ref[...]], axis=1).astype(BF16)
    rowid = lax.broadcasted_iota(jnp.int32, (SUBLANES, tn), 0)

    def conv_of(cur, shifted):
        acc = cb_ref[...] + cw_ref[conv_k - 1:conv_k, :] * cur
        for d in range(1, conv_k):
            acc = acc + cw_ref[conv_k - 1 - d:conv_k - d, :] * shifted[d - 1]
        return acc

    tail = carry_ref[j]
    for c in range(tm // FFN_ROW_CHUNK):
        r0 = c * FFN_ROW_CHUNK
        gu = jnp.dot(h_ref[r0:r0 + FFN_ROW_CHUNK, :], w, preferred_element_type=F32)
        gate = gu[:, :tn]
        up = gu[:, tn:]
        conv = conv_of(gate, [pltpu.roll(gate, d, axis=0) for d in range(1, conv_k)])
        out_ref[r0:r0 + FFN_ROW_CHUNK, :] = (_gelu_tanh(conv) * up).astype(out_ref.dtype)
        top = gate[:SUBLANES, :]
        top_shifted = [jnp.where(rowid < d, pltpu.roll(tail, d, axis=0), pltpu.roll(top, d, axis=0))
                       for d in range(1, conv_k)]
        out_ref[r0:r0 + SUBLANES, :] = (_gelu_tanh(conv_of(top, top_shifted))
                                        * up[:SUBLANES, :]).astype(out_ref.dtype)
        tail = gate[FFN_ROW_CHUNK - SUBLANES:, :]
    carry_ref[j] = tail


def _ffn_up(h, w_gate, w_up, conv_w, conv_b, layer, tm, tn):
    s, d = h.shape
    ff = w_gate.shape[2]
    conv_k = conv_w.shape[1]
    return pl.pallas_call(
        functools.partial(_ffn_up_kernel, conv_k=conv_k),
        grid=(s // tm, ff // tn),
        in_specs=[pl.BlockSpec((tm, d), lambda i, j: (i, 0)),
                  pl.BlockSpec((None, d, tn), lambda i, j: (layer, 0, j)),
                  pl.BlockSpec((None, d, tn), lambda i, j: (layer, 0, j)),
                  pl.BlockSpec((None, conv_k, tn), lambda i, j: (layer, 0, j)),
                  pl.BlockSpec((None, 1, tn), lambda i, j: (layer, 0, j))],
        out_specs=pl.BlockSpec((tm, tn), lambda i, j: (i, j)),
        out_shape=jax.ShapeDtypeStruct((s, ff), BF16),
        scratch_shapes=[pltpu.VMEM((ff // tn, SUBLANES, tn), F32)],
        compiler_params=_params(2),
        name="ffn_up",
    )(h, w_gate, w_up, conv_w, conv_b.reshape(conv_b.shape[0], 1, ff))


def _ffn_down_kernel(a_ref, w_ref, ar_ref, wr_ref, o_ref, acc_ref):
    k = pl.program_id(1)

    @pl.when(k == 0)
    def _():
        acc_ref[...] = jnp.dot(ar_ref[...], wr_ref[...].astype(BF16), preferred_element_type=F32)

    acc_ref[...] += jnp.dot(a_ref[...], w_ref[...].astype(BF16), preferred_element_type=F32)

    @pl.when(k == pl.num_programs(1) - 1)
    def _():
        o_ref[...] = acc_ref[...].astype(o_ref.dtype)


def _ffn_down(hid, w_down, layer, tm, tk, rem):
    s, ff = hid.shape
    d = w_down.shape[2]
    assert 0 < rem < tk and (ff - rem) % tk == 0 and ff % rem == 0
    rem_blk = ff // rem - 1
    once = pl.Buffered(1)
    return pl.pallas_call(
        _ffn_down_kernel,
        grid=(s // tm, (ff - rem) // tk),
        in_specs=[pl.BlockSpec((tm, tk), lambda i, k: (i, k)),
                  pl.BlockSpec((None, tk, d), lambda i, k: (layer, k, 0)),
                  pl.BlockSpec((tm, rem), lambda i, k: (i, rem_blk), pipeline_mode=once),
                  pl.BlockSpec((None, rem, d), lambda i, k: (layer, rem_blk, 0), pipeline_mode=once)],
        out_specs=pl.BlockSpec((tm, d), lambda i, k: (i, 0)),
        out_shape=jax.ShapeDtypeStruct((s, d), BF16),
        scratch_shapes=[pltpu.VMEM((tm, d), F32)],
        compiler_params=_params(2),
        name="ffn_down",
    )(hid, w_down, hid, w_down)


def _pad_to(a, axis, size):
    pad = [(0, 0)] * a.ndim
    pad[axis] = (0, size - a.shape[axis])
    return jnp.pad(a, pad)


def kernel(x, g_mix_pre, w_in, b_i, b_f, g_mlstm_head, lru_conv_w, lru_conv_b, w_a, b_a, w_x, b_x,
           lam, w_out, g_mix_post, g_ffn_pre, w_gate, w_up, ffn_conv_w, ffn_conv_b, w_down, g_ffn_post):
    bsz, seq, d_model = x.shape
    depth = w_in.shape[0]
    heads = b_i.shape[1]
    lru_blocks, lru_blk = w_a.shape[1], w_a.shape[2]
    d_lru = lru_blocks * lru_blk
    d_mlstm = w_out.shape[1] - d_lru
    dv = d_mlstm // heads
    dk = dv // 2
    qk = heads * dk
    d_ff = w_gate.shape[2]
    gate_off = 2 * qk + 2 * d_mlstm
    assert w_in.shape[2] == gate_off + 2 * heads + 2 * d_lru
    assert d_mlstm == 2 * qk and d_lru == d_mlstm and 2 * heads <= LANES
    s = bsz * seq
    assert bsz == 1, "the recurrences carry state across row tiles of a single sequence"

    tm = min(ROW_TILE, s)
    tm_big = min(2 * ROW_TILE, s)
    assert tm_big % FFN_ROW_CHUNK == 0
    chunk = min(MLSTM_CHUNK, s)
    lru_tile = min(LRU_TILE, s)
    norm_tm = min(256, s)
    n_main = gate_off
    assert n_main % MM_TILE_N == 0 and (2 * d_lru) % MM_TILE_N == 0 and d_model % MM_TILE_N == 0
    assert d_ff % FF_TILE == 0 and d_ff % (2 * FF_TILE) == FF_TILE and gate_off % LANES == 0

    w_in_t = jnp.swapaxes(w_in, 1, 2)
    gate_bias = _pad_to(jnp.concatenate([b_i, b_f], axis=1), 1, LANES)
    w_cat = jnp.concatenate([w_a, w_x], axis=-1).astype(BF16)

    xf = x.reshape(s, d_model)
    row = lambda v: v.reshape(1, -1)
    h = _norm(xf, row(g_mix_pre[0]), norm_tm)
    for l in range(depth):
        zm = _matmul_nt(h, w_in_t, l, 0, n_main, BF16, tm_big, MM_TILE_N, "in_proj_mlstm")
        gates = _matmul_nt(h, w_in_t, l, gate_off, LANES, F32, tm, LANES, "gate_proj")
        zl = _matmul_nt(h, w_in_t, l, gate_off + 2 * heads, 2 * d_lru, BF16, tm_big, MM_TILE_N, "in_proj_lru")
        hm = _mlstm(zm, gates, row(gate_bias[l]), row(g_mlstm_head[l]), heads, dk, dv, chunk)
        hl = _rglru(zl, lru_conv_w[l], row(lru_conv_b[l]), w_cat[l],
                    row(b_a[l]), row(b_x[l]), row(lam[l]), lru_tile)
        mix = _matmul([(hm, w_out, l, 0, 0), (hl, w_out, l, 1, 0)], d_model, BF16, tm_big, MM_TILE_N,
                      "out_proj")
        xf, h = _resnorm(xf, mix, row(g_mix_post[l]), row(g_ffn_pre[l]), norm_tm)

        hid = _ffn_up(h, w_gate, w_up, ffn_conv_w, ffn_conv_b, l, tm_big, FF_TILE)
        ff = _ffn_down(hid, w_down, l, tm, 2 * FF_TILE, FF_TILE)
        g_next = row(g_mix_pre[l + 1]) if l + 1 < depth else None
        xf, h = _resnorm(xf, ff, row(g_ffn_post[l]), g_next, norm_tm)
    return xf.reshape(bsz, seq, d_model)
```

```python
import functools
import math

import jax
import jax.numpy as jnp
from jax import lax
from jax.experimental import pallas as pl
from jax.experimental.pallas import tpu as pltpu

EPS = 1e-6
GATE_SOFTCAP = 15.0
LRU_C = 8.0
LANES = 128
SUBLANES = 8
VMEM_LIMIT_BYTES = 60 * 1024 * 1024
MLSTM_CHUNK = 256
LRU_TILE = 256
FF_TILE = 256
ROW_TILE = 1024
MM_TILE_N = 512
FFN_ROW_CHUNK = 1024

BF16 = jnp.bfloat16
F32 = jnp.float32


def _params(n_axes):
    return pltpu.CompilerParams(
        dimension_semantics=("arbitrary",) * n_axes,
        vmem_limit_bytes=VMEM_LIMIT_BYTES)


def _gelu_tanh(y):
    c = math.sqrt(2.0 / math.pi)
    return 0.5 * y * (1.0 + jnp.tanh(c * (y + 0.044715 * (y * y * y))))


def _sigmoid(y):
    return 0.5 * jnp.tanh(0.5 * y) + 0.5


def _softplus(y):
    return jnp.maximum(y, 0.0) + jnp.log1p(jnp.exp(-jnp.abs(y)))


def _rms_scale(v):
    return lax.rsqrt(jnp.mean(v * v, axis=-1, keepdims=True) + EPS)


def _norm_kernel(x_ref, g_ref, h_ref):
    x = x_ref[...]
    h_ref[...] = (x * _rms_scale(x) * g_ref[...]).astype(h_ref.dtype)


def _norm(x, g, tm):
    s, d = x.shape
    return pl.pallas_call(
        _norm_kernel,
        grid=(s // tm,),
        in_specs=[pl.BlockSpec((tm, d), lambda i: (i, 0)),
                  pl.BlockSpec((1, d), lambda i: (0, 0))],
        out_specs=pl.BlockSpec((tm, d), lambda i: (i, 0)),
        out_shape=jax.ShapeDtypeStruct((s, d), BF16),
        compiler_params=_params(1),
        name="rmsnorm",
    )(x, g)


def _resnorm_kernel(x_ref, y_ref, gp_ref, gn_ref, xo_ref, h_ref):
    y = y_ref[...].astype(F32)
    xn = x_ref[...] + y * _rms_scale(y) * gp_ref[...]
    xo_ref[...] = xn
    h_ref[...] = (xn * _rms_scale(xn) * gn_ref[...]).astype(h_ref.dtype)


def _resnorm_last_kernel(x_ref, y_ref, gp_ref, xo_ref):
    y = y_ref[...].astype(F32)
    xo_ref[...] = x_ref[...] + y * _rms_scale(y) * gp_ref[...]


def _resnorm(x, y, g_post, g_next, tm):
    s, d = x.shape
    row = pl.BlockSpec((tm, d), lambda i: (i, 0))
    vec = pl.BlockSpec((1, d), lambda i: (0, 0))
    if g_next is None:
        return pl.pallas_call(
            _resnorm_last_kernel, grid=(s // tm,),
            in_specs=[row, row, vec], out_specs=row,
            out_shape=jax.ShapeDtypeStruct((s, d), F32),
            compiler_params=_params(1), name="resnorm_last",
        )(x, y, g_post), None
    return pl.pallas_call(
        _resnorm_kernel, grid=(s // tm,),
        in_specs=[row, row, vec, vec], out_specs=[row, row],
        out_shape=[jax.ShapeDtypeStruct((s, d), F32), jax.ShapeDtypeStruct((s, d), BF16)],
        compiler_params=_params(1), name="resnorm",
    )(x, y, g_post, g_next)


def _mm_kernel(*refs, n_pairs):
    o_ref = refs[-1]
    acc = None
    for p in range(n_pairs):
        d = jnp.dot(refs[2 * p][...], refs[2 * p + 1][...].astype(BF16), preferred_element_type=F32)
        acc = d if acc is None else acc + d
    o_ref[...] = acc.astype(o_ref.dtype)


def _matmul(pairs, n_out, out_dtype, tm, tn, name):
    s = pairs[0][0].shape[0]
    in_specs, args = [], []
    for a, w, layer, kblk, jblk0 in pairs:
        k = a.shape[1]
        in_specs += [pl.BlockSpec((tm, k), lambda i, j: (i, 0)),
                     pl.BlockSpec((None, k, tn), lambda i, j, layer=layer, kblk=kblk, jblk0=jblk0:
                                  (layer, kblk, jblk0 + j))]
        args += [a, w]
    return pl.pallas_call(
        functools.partial(_mm_kernel, n_pairs=len(pairs)),
        grid=(s // tm, n_out // tn),
        in_specs=in_specs,
        out_specs=pl.BlockSpec((tm, tn), lambda i, j: (i, j)),
        out_shape=jax.ShapeDtypeStruct((s, n_out), out_dtype),
        compiler_params=_params(2),
        name=name,
    )(*args)


def _mm_nt_kernel(a_ref, w_ref, o_ref):
    o_ref[...] = lax.dot_general(a_ref[...], w_ref[0].astype(BF16), (((1,), (1,)), ((), ())),
                                 preferred_element_type=F32).astype(o_ref.dtype)


def _matmul_nt(a, w_t, layer, row0, n_out, out_dtype, tm, tn, name):
    s, k = a.shape
    assert row0 % SUBLANES == 0 and n_out % tn == 0
    return pl.pallas_call(
        _mm_nt_kernel,
        grid=(s // tm, n_out // tn),
        in_specs=[pl.BlockSpec((tm, k), lambda i, j: (i, 0)),
                  pl.BlockSpec((pl.Element(1), pl.Element(tn), pl.Element(k)),
                               lambda i, j: (layer, pl.multiple_of(row0 + j * tn, SUBLANES), 0))],
        out_specs=pl.BlockSpec((tm, tn), lambda i, j: (i, j)),
        out_shape=jax.ShapeDtypeStruct((s, n_out), out_dtype),
        compiler_params=_params(2),
        name=name,
    )(a, w_t)


def _mlstm_kernel(q_ref, k_ref, v_ref, o_ref, gt_ref, gb_ref, gh_ref, out_ref,
                  c_ref, n_ref, m_ref, *, heads, dk, dv):
    chunk = q_ref.shape[0]
    scale = dk ** -0.5

    @pl.when(pl.program_id(0) == 0)
    def _():
        c_ref[...] = jnp.zeros_like(c_ref)
        n_ref[...] = jnp.zeros_like(n_ref)
        m_ref[...] = jnp.zeros_like(m_ref)

    pre = gt_ref[...] + gb_ref[...]
    capped = GATE_SOFTCAP * jnp.tanh(pre * (1.0 / GATE_SOFTCAP))
    logf = -_softplus(-capped)
    rows = lax.broadcasted_iota(jnp.int32, (chunk, chunk), 0)
    cols = lax.broadcasted_iota(jnp.int32, (chunk, chunk), 1)
    causal = rows >= cols
    tri = jnp.where(causal, 1.0, 0.0).astype(BF16)
    p0 = logf.astype(BF16)
    r1 = logf - p0.astype(F32)
    p1 = r1.astype(BF16)
    p2 = (r1 - p1.astype(F32)).astype(BF16)
    bcum = (jnp.dot(tri, p0, preferred_element_type=F32)
            + jnp.dot(tri, p1, preferred_element_type=F32)
            + jnp.dot(tri, p2, preferred_element_type=F32))
    wmat = capped - pltpu.roll(bcum, LANES - heads, axis=1)
    wmat_t = wmat.T

    for h in range(heads):
        q = q_ref[:, h * dk:(h + 1) * dk]
        k = k_ref[:, h * dk:(h + 1) * dk]
        v = v_ref[:, h * dv:(h + 1) * dv]
        b_col = bcum[:, heads + h:heads + h + 1]
        w_col = wmat[:, h:h + 1]
        w_row = wmat_t[h:h + 1, :]
        m_prev = m_ref[h, :, 0:1]
        c_prev = c_ref[h]
        n_prev = n_ref[h]

        log_d = jnp.where(causal, b_col + w_row, -jnp.inf)
        m_inter = b_col + m_prev
        m_t = jnp.maximum(m_inter, jnp.max(log_d, axis=1, keepdims=True))
        dmat = jnp.exp(log_d - m_t)
        s_qk = lax.dot_general(q, k, (((1,), (1,)), ((), ())), preferred_element_type=F32)
        scores = s_qk * dmat * scale
        inter = jnp.exp(m_inter - m_t) * scale
        num = inter * jnp.dot(q, c_prev.astype(BF16), preferred_element_type=F32) \
            + jnp.dot(scores.astype(BF16), v, preferred_element_type=F32)
        den = inter * jnp.sum(q.astype(F32) * n_prev, axis=1, keepdims=True) \
            + jnp.sum(scores, axis=1, keepdims=True)
        hval = num * (1.0 / jnp.maximum(jnp.abs(den), jnp.exp(-m_t)))
        hval = hval * _rms_scale(hval) * gh_ref[:, h * dv:(h + 1) * dv]
        gate = _sigmoid(o_ref[:, h * dv:(h + 1) * dv].astype(F32))
        out_ref[:, h * dv:(h + 1) * dv] = (hval * gate).astype(out_ref.dtype)

        b_last = b_col[chunk - 1:chunk, :]
        m_new = jnp.maximum(b_last + m_prev,
                            b_last + jnp.max(w_col, axis=0, keepdims=True))
        decay = jnp.exp(b_last + m_prev - m_new)
        wk = jnp.exp(b_last + w_col - m_new)
        kw = k.astype(F32) * wk
        c_ref[h] = decay * c_prev + lax.dot_general(
            kw.astype(BF16), v, (((0,), (0,)), ((), ())), preferred_element_type=F32)
        n_ref[h] = decay * n_prev + jnp.sum(kw, axis=0, keepdims=True)
        m_ref[h] = jnp.broadcast_to(m_new, (1, LANES))


def _mlstm(z, gates, gate_bias, g_head, heads, dk, dv, chunk):
    s = z.shape[0]
    qk = heads * dk
    dm = heads * dv
    return pl.pallas_call(
        functools.partial(_mlstm_kernel, heads=heads, dk=dk, dv=dv),
        grid=(s // chunk,),
        in_specs=[pl.BlockSpec((chunk, qk), lambda c: (c, 0)),
                  pl.BlockSpec((chunk, qk), lambda c: (c, 1)),
                  pl.BlockSpec((chunk, dm), lambda c: (c, 1)),
                  pl.BlockSpec((chunk, dm), lambda c: (c, 2)),
                  pl.BlockSpec((chunk, LANES), lambda c: (c, 0)),
                  pl.BlockSpec((1, LANES), lambda c: (0, 0)),
                  pl.BlockSpec((1, dm), lambda c: (0, 0))],
        out_specs=pl.BlockSpec((chunk, dm), lambda c: (c, 0)),
        out_shape=jax.ShapeDtypeStruct((s, dm), BF16),
        scratch_shapes=[pltpu.VMEM((heads, dk, dv), F32),
                        pltpu.VMEM((heads, 1, dk), F32),
                        pltpu.VMEM((heads, 1, LANES), F32)],
        compiler_params=_params(1),
        name="mlstm",
    )(z, z, z, z, gates, gate_bias, g_head)


def _rglru_kernel(zx_ref, zy_ref, cw_ref, cb_ref, w_ref, ba_ref, bx_ref, lam_ref, out_ref,
                  xs_ref, a_ref, u_ref, h_ref, hc_ref, *, blocks, conv_k):
    tile, width = zx_ref.shape
    blk = width // blocks

    @pl.when(pl.program_id(0) == 0)
    def _():
        xs_ref[0:SUBLANES, :] = jnp.zeros((SUBLANES, width), F32)
        hc_ref[...] = jnp.zeros_like(hc_ref)

    x = zx_ref[...].astype(F32)
    xs_ref[SUBLANES:SUBLANES + tile, :] = x
    xc = cb_ref[...] + cw_ref[conv_k - 1:conv_k, :] * x
    for j in range(conv_k - 1):
        off = SUBLANES - (conv_k - 1) + j
        xc = xc + cw_ref[j:j + 1, :] * xs_ref[off:off + tile, :]
    xs_ref[0:SUBLANES, :] = x[tile - SUBLANES:, :]

    sp = _softplus(-lam_ref[...])
    for n in range(blocks):
        sl = slice(n * blk, (n + 1) * blk)
        xb = xc[:, sl]
        ri = jnp.dot(xb.astype(BF16), w_ref[n], preferred_element_type=F32)
        r = _sigmoid(ri[:, :blk] + ba_ref[:, sl])
        ig = _sigmoid(ri[:, blk:] + bx_ref[:, sl])
        log_a = (-LRU_C) * r * sp[:, sl]
        a_ref[:, sl] = jnp.exp(log_a)
        th = jnp.tanh(log_a)
        u_ref[:, sl] = jnp.sqrt(-2.0 * th / (1.0 - th)) * (ig * xb)

    rowid = lax.broadcasted_iota(jnp.int32, (SUBLANES, width), 0)

    def group(g, hc):
        r0 = pl.multiple_of(g * SUBLANES, SUBLANES)
        a = a_ref[pl.ds(r0, SUBLANES), :]
        u = u_ref[pl.ds(r0, SUBLANES), :]
        for d in (1, 2, 4):
            keep = rowid >= d
            a_sh = jnp.where(keep, pltpu.roll(a, d, axis=0), 1.0)
            u_sh = jnp.where(keep, pltpu.roll(u, d, axis=0), 0.0)
            u = a * u_sh + u
            a = a * a_sh
        hv = u + a * hc
        h_ref[pl.ds(r0, SUBLANES), :] = hv
        return jnp.broadcast_to(hv[SUBLANES - 1:SUBLANES, :], (SUBLANES, width))

    hc = lax.fori_loop(0, tile // SUBLANES, group, hc_ref[...])
    hc_ref[...] = hc
    out_ref[...] = (h_ref[...] * _gelu_tanh(zy_ref[...].astype(F32))).astype(out_ref.dtype)


def _rglru(z, conv_w, conv_b, w_cat, b_a, b_x, lam, tile):
    s = z.shape[0]
    blocks, blk, _ = w_cat.shape
    width = blocks * blk
    conv_k = conv_w.shape[0]
    vec = pl.BlockSpec((1, width), lambda i: (0, 0))
    return pl.pallas_call(
        functools.partial(_rglru_kernel, blocks=blocks, conv_k=conv_k),
        grid=(s // tile,),
        in_specs=[pl.BlockSpec((tile, width), lambda i: (i, 0)),
                  pl.BlockSpec((tile, width), lambda i: (i, 1)),
                  pl.BlockSpec((conv_k, width), lambda i: (0, 0)),
                  vec,
                  pl.BlockSpec((blocks, blk, 2 * blk), lambda i: (0, 0, 0)),
                  vec, vec, vec],
        out_specs=pl.BlockSpec((tile, width), lambda i: (i, 0)),
        out_shape=jax.ShapeDtypeStruct((s, width), BF16),
        scratch_shapes=[pltpu.VMEM((tile + SUBLANES, width), F32),
                        pltpu.VMEM((tile, width), F32),
                        pltpu.VMEM((tile, width), F32),
                        pltpu.VMEM((tile, width), F32),
                        pltpu.VMEM((SUBLANES, width), F32)],
        compiler_params=_params(1),
        name="rglru",
    )(z, z, conv_w, conv_b, w_cat, b_a, b_x, lam)


def _ffn_up_kernel(h_ref, wg_ref, wu_ref, cw_ref, cb_ref, out_ref, carry_ref, *, conv_k):
    i = pl.program_id(0)
    j = pl.program_id(1)
    tm = h_ref.shape[0]
    tn = out_ref.shape[1]

    @pl.when(i == 0)
    def _():
        carry_ref[j] = jnp.zeros((SUBLANES, tn), F32)

    w = jnp.concatenate([wg_ref[...], wu_ref[...]], axis=1).astype(BF16)
    rowid = lax.broadcasted_iota(jnp.int32, (SUBLANES, tn), 0)

    def conv_of(cur, shifted):
        acc = cb_ref[...] + cw_ref[conv_k - 1:conv_k, :] * cur
        for d in range(1, conv_k):
            acc = acc + cw_ref[conv_k - 1 - d:conv_k - d, :] * shifted[d - 1]
        return acc

    tail = carry_ref[j]
    for c in range(tm // FFN_ROW_CHUNK):
        r0 = c * FFN_ROW_CHUNK
        gu = jnp.dot(h_ref[r0:r0 + FFN_ROW_CHUNK, :], w, preferred_element_type=F32)
        gate = gu[:, :tn]
        up = gu[:, tn:]
        conv = conv_of(gate, [pltpu.roll(gate, d, axis=0) for d in range(1, conv_k)])
        out_ref[r0:r0 + FFN_ROW_CHUNK, :] = (_gelu_tanh(conv) * up).astype(out_ref.dtype)
        top = gate[:SUBLANES, :]
        top_shifted = [jnp.where(rowid < d, pltpu.roll(tail, d, axis=0), pltpu.roll(top, d, axis=0))
                       for d in range(1, conv_k)]
        out_ref[r0:r0 + SUBLANES, :] = (_gelu_tanh(conv_of(top, top_shifted))
                                        * up[:SUBLANES, :]).astype(out_ref.dtype)
        tail = gate[FFN_ROW_CHUNK - SUBLANES:, :]
    carry_ref[j] = tail


def _ffn_up(h, w_gate, w_up, conv_w, conv_b, layer, tm, tn):
    s, d = h.shape
    ff = w_gate.shape[2]
    conv_k = conv_w.shape[1]
    return pl.pallas_call(
        functools.partial(_ffn_up_kernel, conv_k=conv_k),
        grid=(s // tm, ff // tn),
        in_specs=[pl.BlockSpec((tm, d), lambda i, j: (i, 0)),
                  pl.BlockSpec((None, d, tn), lambda i, j: (layer, 0, j)),
                  pl.BlockSpec((None, d, tn), lambda i, j: (layer, 0, j)),
                  pl.BlockSpec((None, conv_k, tn), lambda i, j: (layer, 0, j)),
                  pl.BlockSpec((None, 1, tn), lambda i, j: (layer, 0, j))],
        out_specs=pl.BlockSpec((tm, tn), lambda i, j: (i, j)),
        out_shape=jax.ShapeDtypeStruct((s, ff), BF16),
        scratch_shapes=[pltpu.VMEM((ff // tn, SUBLANES, tn), F32)],
        compiler_params=_params(2),
        name="ffn_up",
    )(h, w_gate, w_up, conv_w, conv_b.reshape(conv_b.shape[0], 1, ff))


def _ffn_down_kernel(a_ref, w_ref, ar_ref, wr_ref, o_ref, acc_ref):
    k = pl.program_id(1)

    @pl.when(k == 0)
    def _():
        acc_ref[...] = jnp.dot(ar_ref[...], wr_ref[...].astype(BF16), preferred_element_type=F32)

    acc_ref[...] += jnp.dot(a_ref[...], w_ref[...].astype(BF16), preferred_element_type=F32)

    @pl.when(k == pl.num_programs(1) - 1)
    def _():
        o_ref[...] = acc_ref[...].astype(o_ref.dtype)


def _ffn_down(hid, w_down, layer, tm, tk, rem):
    s, ff = hid.shape
    d = w_down.shape[2]
    assert 0 < rem < tk and (ff - rem) % tk == 0 and ff % rem == 0
    rem_blk = ff // rem - 1
    once = pl.Buffered(1)
    return pl.pallas_call(
        _ffn_down_kernel,
        grid=(s // tm, (ff - rem) // tk),
        in_specs=[pl.BlockSpec((tm, tk), lambda i, k: (i, k)),
                  pl.BlockSpec((None, tk, d), lambda i, k: (layer, k, 0)),
                  pl.BlockSpec((tm, rem), lambda i, k: (i, rem_blk), pipeline_mode=once),
                  pl.BlockSpec((None, rem, d), lambda i, k: (layer, rem_blk, 0), pipeline_mode=once)],
        out_specs=pl.BlockSpec((tm, d), lambda i, k: (i, 0)),
        out_shape=jax.ShapeDtypeStruct((s, d), BF16),
        scratch_shapes=[pltpu.VMEM((tm, d), F32)],
        compiler_params=_params(2),
        name="ffn_down",
    )(hid, w_down, hid, w_down)


def _pad_to(a, axis, size):
    pad = [(0, 0)] * a.ndim
    pad[axis] = (0, size - a.shape[axis])
    return jnp.pad(a, pad)


def kernel(x, g_mix_pre, w_in, b_i, b_f, g_mlstm_head, lru_conv_w, lru_conv_b, w_a, b_a, w_x, b_x,
           lam, w_out, g_mix_post, g_ffn_pre, w_gate, w_up, ffn_conv_w, ffn_conv_b, w_down, g_ffn_post):
    bsz, seq, d_model = x.shape
    depth = w_in.shape[0]
    heads = b_i.shape[1]
    lru_blocks, lru_blk = w_a.shape[1], w_a.shape[2]
    d_lru = lru_blocks * lru_blk
    d_mlstm = w_out.shape[1] - d_lru
    dv = d_mlstm // heads
    dk = dv // 2
    qk = heads * dk
    d_ff = w_gate.shape[2]
    gate_off = 2 * qk + 2 * d_mlstm
    assert w_in.shape[2] == gate_off + 2 * heads + 2 * d_lru
    assert d_mlstm == 2 * qk and d_lru == d_mlstm and 2 * heads <= LANES
    s = bsz * seq
    assert bsz == 1, "the recurrences carry state across row tiles of a single sequence"

    tm = min(ROW_TILE, s)
    tm_big = min(2 * ROW_TILE, s)
    assert tm_big % FFN_ROW_CHUNK == 0
    chunk = min(MLSTM_CHUNK, s)
    lru_tile = min(LRU_TILE, s)
    norm_tm = min(256, s)
    n_main = gate_off
    assert n_main % MM_TILE_N == 0 and (2 * d_lru) % MM_TILE_N == 0 and d_model % MM_TILE_N == 0
    assert d_ff % FF_TILE == 0 and d_ff % (2 * FF_TILE) == FF_TILE and gate_off % LANES == 0

    w_in_t = jnp.swapaxes(w_in, 1, 2)
    gate_bias = _pad_to(jnp.concatenate([b_i, b_f], axis=1), 1, LANES)
    w_cat = jnp.concatenate([w_a, w_x], axis=-1).astype(BF16)

    xf = x.reshape(s, d_model)
    row = lambda v: v.reshape(1, -1)
    h = _norm(xf, row(g_mix_pre[0]), norm_tm)
    for l in range(depth):
        zm = _matmul_nt(h, w_in_t, l, 0, n_main, BF16, tm_big, MM_TILE_N, "in_proj_mlstm")
        gates = _matmul_nt(h, w_in_t, l, gate_off, LANES, F32, tm, LANES, "gate_proj")
        zl = _matmul_nt(h, w_in_t, l, gate_off + 2 * heads, 2 * d_lru, BF16, tm_big, MM_TILE_N, "in_proj_lru")
        hm = _mlstm(zm, gates, row(gate_bias[l]), row(g_mlstm_head[l]), heads, dk, dv, chunk)
        hl = _rglru(zl, lru_conv_w[l], row(lru_conv_b[l]), w_cat[l],
                    row(b_a[l]), row(b_x[l]), row(lam[l]), lru_tile)
        mix = _matmul([(hm, w_out, l, 0, 0), (hl, w_out, l, 1, 0)], d_model, BF16, tm_big, MM_TILE_N,
                      "out_proj")
        xf, h = _resnorm(xf, mix, row(g_mix_post[l]), row(g_ffn_pre[l]), norm_tm)

        hid = _ffn_up(h, w_gate, w_up, ffn_conv_w, ffn_conv_b, l, tm_big, FF_TILE)
        ff = _ffn_down(hid, w_down, l, tm, 2 * FF_TILE, FF_TILE)
        g_next = row(g_mix_pre[l + 1]) if l + 1 < depth else None
        xf, h = _resnorm(xf, ff, row(g_ffn_post[l]), g_next, norm_tm)
    return xf.reshape(bsz, seq, d_model)
```
